```python
import jax, jax.numpy as jnp
from jax import lax
import numpy as np

D_MODEL = 1024
BATCH = 4
SEQ = 4096
DEPTH = 1
DEC_BATCH = 32
DEC_SEQ = 64
PAST_LEN = 2048

CHUNK = 64
Q_BLOCK = 128
EPS = 1e-5
CONV_CH = 512
CONV_WIDTH = 31
MLA_HEADS = 8
Q_LORA = 384
KV_LORA = 256
QK_NOPE = 64
QK_ROPE = 32
V_HEAD = 64
ROPE_BASE = 10000.0
MLA_SCALE = (QK_NOPE + QK_ROPE) ** -0.5
MEM_TOKENS = 256
MEM_HEADS = 4
MEM_HEAD_DIM = 128
MEM_SCALE = MEM_HEAD_DIM ** -0.5
N_BRANCH = 3
N_GROUPS = 4
EXPERTS_PER_GROUP = 8
TOP_K = 2
D_EXPERT = 256
DEEPNORM_ALPHA = (2.0 * DEPTH) ** 0.25
DEEPNORM_BETA = (8.0 * DEPTH) ** -0.25
IN_GLU = 2 * CONV_CH
IN_QDOWN = Q_LORA
IN_KVDOWN = KV_LORA + QK_ROPE
IN_MEMQ = MEM_HEADS * MEM_HEAD_DIM
IN_GATE = N_BRANCH * D_MODEL
D_IN = IN_GLU + IN_QDOWN + IN_KVDOWN + IN_MEMQ + IN_GATE
SPLIT_POINTS = (IN_GLU, IN_GLU + IN_QDOWN, IN_GLU + IN_QDOWN + IN_KVDOWN, IN_GLU + IN_QDOWN + IN_KVDOWN + IN_MEMQ)

kernel_name = 'hybrid_conv_mla_memx_hmoe_stream'


def layer_norm(x, g, b):
    xf = x.astype(jnp.float32)
    mu = jnp.mean(xf, axis=-1, keepdims=True)
    var = jnp.mean(jnp.square(xf - mu), axis=-1, keepdims=True)
    return ((xf - mu) * lax.rsqrt(var + EPS) * g + b).astype(x.dtype)


def rms_norm(x, g):
    xf = x.astype(jnp.float32)
    return (xf * lax.rsqrt(jnp.mean(xf * xf, axis=-1, keepdims=True) + EPS) * g).astype(x.dtype)


def rope_tables(pos):
    half = QK_ROPE // 2
    inv = jnp.power(ROPE_BASE, -jnp.arange(half, dtype=jnp.float32) / half)
    ang = pos.astype(jnp.float32)[:, None] * inv[None, :]
    return jnp.cos(ang), jnp.sin(ang)


def apply_rope(x, cos, sin):
    half = QK_ROPE // 2
    x1 = x[..., :half].astype(jnp.float32)
    x2 = x[..., half:].astype(jnp.float32)
    return jnp.concatenate([x1 * cos - x2 * sin, x1 * sin + x2 * cos], axis=-1).astype(x.dtype)


def conv_branch(glu_in, conv_state, w_dw, b_dw, g_cn, b_cn, w_co):
    a, b = jnp.split(glu_in, 2, axis=-1)
    u = a * jax.nn.sigmoid(b)
    padded = jnp.concatenate([conv_state.astype(u.dtype), u], axis=1)
    y = lax.conv_general_dilated(padded, w_dw[:, None, :].astype(u.dtype), window_strides=(1,), padding='VALID',
                                 dimension_numbers=('NWC', 'WIO', 'NWC'), feature_group_count=CONV_CH) + b_dw
    y = jax.nn.silu(layer_norm(y, g_cn, b_cn))
    return y @ w_co, padded[:, -(CONV_WIDTH - 1):]


def mla_attend(q_lat, q_rot, c_kv, k_rot, q_pos, k_pos):
    s = jnp.einsum('nqhc,nkc->nhqk', q_lat, c_kv) + jnp.einsum('nqhr,nkr->nhqk', q_rot, k_rot)
    s = s.astype(jnp.float32) * MLA_SCALE
    mask = (k_pos[None, :] // CHUNK) <= (q_pos[:, None] // CHUNK)
    s = jnp.where(mask[None, None], s, -1e30)
    p = jax.nn.softmax(s, axis=-1).astype(c_kv.dtype)
    return jnp.einsum('nhqk,nkc->nqhc', p, c_kv)


def mem_attend(q, mem_k, mem_v):
    s = jnp.einsum('nthd,nmhd->nhtm', q, mem_k).astype(jnp.float32) * MEM_SCALE
    p = jax.nn.softmax(s, axis=-1).astype(mem_v.dtype)
    return jnp.einsum('nhtm,nmhd->nthd', p, mem_v)


def hier_moe(x, w_group, b_group, w_router, b_router, w_e1, w_e3, w_e2):
    n, t, d = x.shape
    xt = x.reshape(n * t, d)
    p_group = jax.nn.softmax((xt @ w_group + b_group).astype(jnp.float32), axis=-1)
    g_val, g_idx = lax.top_k(p_group, 1)
    g_onehot = jax.nn.one_hot(g_idx[:, 0], N_GROUPS, dtype=jnp.float32)
    e_logits = (xt @ w_router + b_router).astype(jnp.float32).reshape(n * t, N_GROUPS, EXPERTS_PER_GROUP)
    e_sel = jnp.einsum('mge,mg->me', e_logits, g_onehot)
    p_exp = jax.nn.softmax(e_sel, axis=-1)
    e_val, e_idx = lax.top_k(p_exp, TOP_K)
    e_val = e_val / jnp.sum(e_val, axis=-1, keepdims=True)
    w_in_group = jnp.sum(jax.nn.one_hot(e_idx, EXPERTS_PER_GROUP, dtype=jnp.float32) * e_val[..., None], axis=1)
    combine = ((g_val * g_onehot)[:, :, None] * w_in_group[:, None, :]).astype(x.dtype)
    out = jnp.zeros_like(xt)
    for gi in range(N_GROUPS):
        h = jax.nn.silu(jnp.einsum('md,edf->mef', xt, w_e1[gi])) * jnp.einsum('md,edf->mef', xt, w_e3[gi])
        out = out + jnp.einsum('mef,efd->md', h * combine[:, gi, :, None], w_e2[gi])
    return out.reshape(n, t, d)


def trunk_layer(x, pos, conv_state, past_lat, past_krot, mem_k, mem_v,
                w_in, b_gate, w_dw, b_dw, g_cn, b_cn, w_co,
                g_qa, w_uq, g_kva, w_uk, w_uv, w_mo, w_mem_o, w_out, g_ln1, b_ln1,
                w_group, b_group, w_router, b_router, w_e1, w_e3, w_e2, g_ln2, b_ln2):
    n, t, _ = x.shape
    proj = x @ w_in
    glu_in, q_down, kv_down, mem_q, gate_pre = jnp.split(proj, SPLIT_POINTS, axis=-1)
    y_conv, new_conv = conv_branch(glu_in, conv_state, w_dw, b_dw, g_cn, b_cn, w_co)
    cos, sin = rope_tables(pos)
    q = (rms_norm(q_down, g_qa) @ w_uq).reshape(n, t, MLA_HEADS, QK_NOPE + QK_ROPE)
    q_lat = jnp.einsum('nthd,chd->nthc', q[..., :QK_NOPE], w_uk)
    q_rot = apply_rope(q[..., QK_NOPE:], cos[:, None, :], sin[:, None, :])
    c_kv = rms_norm(kv_down[..., :KV_LORA], g_kva)
    k_rot = apply_rope(kv_down[..., KV_LORA:], cos, sin)
    if past_lat is None:
        nblk = t // Q_BLOCK
        to_blocks = lambda a: jnp.swapaxes(a.reshape((n, nblk, Q_BLOCK) + a.shape[2:]), 0, 1)
        o_lat = lax.map(lambda blk: mla_attend(blk[0], blk[1], c_kv, k_rot, blk[2], pos),
                        (to_blocks(q_lat), to_blocks(q_rot), pos.reshape(nblk, Q_BLOCK)))
        o_lat = jnp.swapaxes(o_lat, 0, 1).reshape(n, t, MLA_HEADS, KV_LORA)
    else:
        keys_lat = jnp.concatenate([past_lat, c_kv], axis=1)
        keys_rot = jnp.concatenate([past_krot, k_rot], axis=1)
        k_pos = jnp.arange(keys_lat.shape[1], dtype=jnp.int32)
        o_lat = mla_attend(q_lat, q_rot, keys_lat, keys_rot, pos, k_pos)
    y_mla = jnp.einsum('nthc,chv->nthv', o_lat, w_uv).reshape(n, t, MLA_HEADS * V_HEAD) @ w_mo
    y_mem = mem_attend(mem_q.reshape(n, t, MEM_HEADS, MEM_HEAD_DIM), mem_k, mem_v).reshape(n, t, IN_MEMQ) @ w_mem_o
    gates = jax.nn.sigmoid(gate_pre + b_gate).reshape(n, t, N_BRANCH, D_MODEL)
    merged = gates[:, :, 0] * y_conv + gates[:, :, 1] * y_mla + gates[:, :, 2] * y_mem
    x1 = layer_norm(DEEPNORM_ALPHA * x + merged @ w_out, g_ln1, b_ln1)
    y = hier_moe(x1, w_group, b_group, w_router, b_router, w_e1, w_e3, w_e2)
    x2 = layer_norm(DEEPNORM_ALPHA * x1 + y, g_ln2, b_ln2)
    return x2, new_conv, c_kv, k_rot


def setup_inputs(seed: int = 0) -> dict:
    key = jax.random.key(seed)
    ks = iter(jax.random.split(key, 40))
    nrm = lambda shape, scale: scale * jax.random.normal(next(ks), shape, jnp.float32)
    L, D = DEPTH, D_MODEL
    G, E, F = N_GROUPS, EXPERTS_PER_GROUP, D_EXPERT
    return {
        'x_prompt': nrm((BATCH, SEQ, D), 1.0),
        'x_sample': nrm((DEC_BATCH, DEC_SEQ, D), 1.0),
        'cache_conv': nrm((L, DEC_BATCH, CONV_WIDTH - 1, CONV_CH), 0.5),
        'cache_mla_latent': nrm((L, DEC_BATCH, PAST_LEN, KV_LORA), 1.0),
        'cache_mla_krope': nrm((L, DEC_BATCH, PAST_LEN, QK_ROPE), 1.0),
        'cache_mem_k': nrm((L, DEC_BATCH, MEM_TOKENS, MEM_HEADS, MEM_HEAD_DIM), 1.0),
        'cache_mem_v': nrm((L, DEC_BATCH, MEM_TOKENS, MEM_HEADS, MEM_HEAD_DIM), 1.0),
        'mem_prompt': nrm((BATCH, MEM_TOKENS, D), 1.0),
        'w_in': nrm((L, D, D_IN), D ** -0.5),
        'b_gate': nrm((L, IN_GATE), 0.1),
        'w_dw': nrm((L, CONV_WIDTH, CONV_CH), CONV_WIDTH ** -0.5),
        'b_dw': nrm((L, CONV_CH), 0.02),
        'g_cn': 1.0 + nrm((L, CONV_CH), 0.05),
        'b_cn': nrm((L, CONV_CH), 0.02),
        'w_co': nrm((L, CONV_CH, D), CONV_CH ** -0.5),
        'g_qa': 1.0 + nrm((L, Q_LORA), 0.05),
        'w_uq': nrm((L, Q_LORA, MLA_HEADS * (QK_NOPE + QK_ROPE)), Q_LORA ** -0.5),
        'g_kva': 1.0 + nrm((L, KV_LORA), 0.05),
        'w_uk': nrm((L, KV_LORA, MLA_HEADS, QK_NOPE), KV_LORA ** -0.5),
        'w_uv': nrm((L, KV_LORA, MLA_HEADS, V_HEAD), KV_LORA ** -0.5),
        'w_mo': nrm((L, MLA_HEADS * V_HEAD, D), (MLA_HEADS * V_HEAD) ** -0.5),
        'w_mem_k': nrm((L, D, IN_MEMQ), D ** -0.5),
        'w_mem_v': nrm((L, D, IN_MEMQ), D ** -0.5),
        'w_mem_o': nrm((L, IN_MEMQ, D), IN_MEMQ ** -0.5),
        'w_out': nrm((L, D, D), DEEPNORM_BETA * D ** -0.5),
        'g_ln1': 1.0 + nrm((L, D), 0.05),
        'b_ln1': nrm((L, D), 0.02),
        'w_group': nrm((L, D, G), D ** -0.5),
        'b_group': nrm((L, G), 0.01),
        'w_router': nrm((L, D, G * E), D ** -0.5),
        'b_router': nrm((L, G * E), 0.01),
        'w_e1': nrm((L, G, E, D, F), D ** -0.5),
        'w_e3': nrm((L, G, E, D, F), D ** -0.5),
        'w_e2': nrm((L, G, E, F, D), DEEPNORM_BETA * F ** -0.5),
        'g_ln2': 1.0 + nrm((L, D), 0.05),
        'b_ln2': nrm((L, D), 0.02),
    }


def reference(x_prompt, x_sample, cache_conv, cache_mla_latent, cache_mla_krope, cache_mem_k, cache_mem_v,
              mem_prompt, w_in, b_gate, w_dw, b_dw, g_cn, b_cn, w_co, g_qa, w_uq, g_kva, w_uk, w_uv, w_mo,
              w_mem_k, w_mem_v, w_mem_o, w_out, g_ln1, b_ln1, w_group, b_group, w_router, b_router,
              w_e1, w_e3, w_e2, g_ln2, b_ln2):
    n_p, s_p, _ = x_prompt.shape
    n_s, s_s, _ = x_sample.shape
    n_mem = mem_prompt.shape[1]
    past_len = cache_mla_latent.shape[2]
    pos_p = jnp.arange(s_p, dtype=jnp.int32)
    pos_s = past_len + jnp.arange(s_s, dtype=jnp.int32)
    h_p, h_s = x_prompt, x_sample
    conv_p, lat_p, kr_p, mk_p, mv_p = [], [], [], [], []
    conv_s, lat_s, kr_s = [], [], []
    for l in range(DEPTH):
        lw = (w_in[l], b_gate[l], w_dw[l], b_dw[l], g_cn[l], b_cn[l], w_co[l],
              g_qa[l], w_uq[l], g_kva[l], w_uk[l], w_uv[l], w_mo[l], w_mem_o[l], w_out[l], g_ln1[l], b_ln1[l],
              w_group[l], b_group[l], w_router[l], b_router[l], w_e1[l], w_e3[l], w_e2[l], g_ln2[l], b_ln2[l])
        mem_k_l = (mem_prompt @ w_mem_k[l]).reshape(n_p, n_mem, MEM_HEADS, MEM_HEAD_DIM)
        mem_v_l = (mem_prompt @ w_mem_v[l]).reshape(n_p, n_mem, MEM_HEADS, MEM_HEAD_DIM)
        conv0 = jnp.zeros((n_p, CONV_WIDTH - 1, CONV_CH), x_prompt.dtype)
        h_p, c_p, l_p, r_p = trunk_layer(h_p, pos_p, conv0, None, None, mem_k_l, mem_v_l, *lw)
        conv_p.append(c_p); lat_p.append(l_p); kr_p.append(r_p); mk_p.append(mem_k_l); mv_p.append(mem_v_l)
        h_s, c_s, l_s, r_s = trunk_layer(h_s, pos_s, cache_conv[l], cache_mla_latent[l], cache_mla_krope[l],
                                         cache_mem_k[l], cache_mem_v[l], *lw)
        conv_s.append(c_s); lat_s.append(l_s); kr_s.append(r_s)
    return (h_p, h_s,
            jnp.stack(conv_p, axis=0), jnp.stack(lat_p, axis=0), jnp.stack(kr_p, axis=0),
            jnp.stack(mk_p, axis=0), jnp.stack(mv_p, axis=0),
            jnp.stack(conv_s, axis=0), jnp.stack(lat_s, axis=0), jnp.stack(kr_s, axis=0))
```

```python
import functools

import jax
import jax.numpy as jnp
from jax import lax
from jax.experimental import pallas as pl
from jax.experimental.pallas import tpu as pltpu

F32 = jnp.float32
BF16 = jnp.bfloat16
U32 = jnp.uint32
I32 = jnp.int32

DEPTH = 1
CHUNK = 64
EPS = 1e-5
CONV_CH = 512
CONV_WIDTH = 31
HALO = CONV_WIDTH - 1
MLA_HEADS = 8
Q_LORA = 384
KV_LORA = 256
QK_NOPE = 64
QK_ROPE = 32
V_HEAD = 64
ROPE_BASE = 10000.0
MLA_SCALE = (QK_NOPE + QK_ROPE) ** -0.5
MEM_HEADS = 4
MEM_HEAD_DIM = 128
MEM_SCALE = MEM_HEAD_DIM ** -0.5
N_GROUPS = 4
EXPERTS_PER_GROUP = 8
N_EXPERTS = N_GROUPS * EXPERTS_PER_GROUP
DEEPNORM_ALPHA = (2.0 * DEPTH) ** 0.25

SUBLANES = 8
LANES = 128
TOKEN_TILE = 512
HALO_ROWS = 32
CONV_ROWS = 32
ATT_TQ = 256
SAMPLE_KV = 512
ROW_CHUNK = SUBLANES
EXPERT_TILE = 256
SORT_ROWS = 2 * TOKEN_TILE + N_EXPERTS * ROW_CHUNK
PACKED = 512
VMEM_LIMIT = 56 * 1024 * 1024


def _dot(a, b):
    return jnp.dot(a, b, preferred_element_type=F32)


def _dot_nt(a, b):
    return lax.dot_general(a, b, (((1,), (1,)), ((), ())), preferred_element_type=F32)


def _dot_tn(a, b):
    return lax.dot_general(a, b, (((0,), (0,)), ((), ())), preferred_element_type=F32)


def _sigmoid(x):
    return 1.0 / (1.0 + jnp.exp(-x))


def _layer_norm(x, g, b):
    mu = jnp.mean(x, axis=-1, keepdims=True)
    d = x - mu
    var = jnp.mean(d * d, axis=-1, keepdims=True)
    return d * lax.rsqrt(var + EPS) * g + b


def _rms_norm(x, g):
    return x * lax.rsqrt(jnp.mean(x * x, axis=-1, keepdims=True) + EPS) * g


def _full(a):
    nd = a.ndim
    return pl.BlockSpec(a.shape, lambda *_: (0,) * nd)


def _params(n_axes=1, vmem=VMEM_LIMIT):
    return pltpu.CompilerParams(dimension_semantics=("arbitrary",) * n_axes, vmem_limit_bytes=vmem)


def _mem_proj_kernel(m_ref, wk_ref, wv_ref, k_ref, v_ref):
    m = m_ref[0].astype(BF16)
    k_ref[0] = _dot(m, wk_ref[...])
    v_ref[0] = _dot(m, wv_ref[...])


def _mem_proj(mem, wk, wv):
    n, t, d = mem.shape
    dk = wk.shape[1]
    out = jax.ShapeDtypeStruct((n, t, dk), F32)
    return pl.pallas_call(
        _mem_proj_kernel,
        out_shape=(out, out),
        grid=(n,),
        in_specs=[pl.BlockSpec((1, t, d), lambda i: (i, 0, 0)), _full(wk), _full(wv)],
        out_specs=(pl.BlockSpec((1, t, dk), lambda i: (i, 0, 0)),) * 2,
        compiler_params=_params(),
        name="mem_proj",
    )(mem, wk, wv)


C_GLU = 2 * CONV_CH
C_QD = C_GLU
C_KV = C_QD + Q_LORA
C_KR = C_KV + KV_LORA
C_KRS = C_KR + QK_ROPE
C_MQ = 1792
C_END = C_MQ + MEM_HEADS * MEM_HEAD_DIM
Q_NOPE_W = MLA_HEADS * QK_NOPE
Q_ROPE_W = MLA_HEADS * QK_ROPE


def _inproj_kernel(nseg, seg_len, tiles_per_seq, has_state, *refs):
    it = iter(refs)
    x_ref = next(it)
    st_ref = next(it) if has_state else None
    (w_ref, wuq_ref, wdw_ref, bdw_ref, gcn_ref, bcn_ref, gqa_ref, gkva_ref,
     ck_ref, sk_ref, cq_ref, sq_ref,
     cconv_ref, nconv_ref, ckv_ref, kcb_ref, krot_ref, krb_ref, qn_ref, qr_ref, mq_ref,
     win) = it

    xb = x_ref[...].astype(BF16)
    ab = _dot(xb, w_ref[:, 0:C_GLU])
    u = ab[:, :CONV_CH] * _sigmoid(ab[:, CONV_CH:])

    if has_state:
        for s in range(nseg):
            win[s, HALO_ROWS - HALO:HALO_ROWS, :] = st_ref[s]
            win[s, HALO_ROWS:HALO_ROWS + seg_len, :] = u[s * seg_len:(s + 1) * seg_len]
    else:
        j = pl.program_id(0) % tiles_per_seq

        @pl.when(j == 0)
        def _():
            win[0, 0:HALO_ROWS, :] = jnp.zeros((HALO_ROWS, CONV_CH), F32)

        @pl.when(j != 0)
        def _():
            win[0, 0:HALO_ROWS, :] = win[0, seg_len:seg_len + HALO_ROWS, :]

        win[0, HALO_ROWS:HALO_ROWS + seg_len, :] = u

    g_cn = gcn_ref[...]
    b_cn = bcn_ref[...]
    b_dw = bdw_ref[...]
    base = HALO_ROWS - HALO
    for s in range(nseg):
        for rc in range(seg_len // CONV_ROWS):
            r0 = rc * CONV_ROWS
            acc = jnp.zeros((CONV_ROWS // SUBLANES, SUBLANES, CONV_CH), F32) + b_dw
            for k in range(CONV_WIDTH):
                xk = win[s, base + r0 + k:base + r0 + k + CONV_ROWS, :]
                acc = acc + xk.reshape(CONV_ROWS // SUBLANES, SUBLANES, CONV_CH) * wdw_ref[k]
            y = _layer_norm(acc.reshape(CONV_ROWS, CONV_CH), g_cn, b_cn)
            y = y * _sigmoid(y)
            cconv_ref[s * seg_len + r0:s * seg_len + r0 + CONV_ROWS, :] = y.astype(BF16)
        nconv_ref[s] = win[s, seg_len + HALO_ROWS - HALO:seg_len + HALO_ROWS, :]

    rest = _dot(xb, w_ref[:, C_QD:C_END])
    o = -C_QD
    qd = rest[:, C_QD + o:C_KV + o]
    qn = _rms_norm(qd, gqa_ref[...]).astype(BF16)
    q = _dot(qn, wuq_ref[...])
    qn_ref[...] = (q[:, :Q_NOPE_W] * MLA_SCALE).astype(BF16)
    qr_ref[...] = (q[:, Q_NOPE_W:Q_NOPE_W + Q_ROPE_W] * cq_ref[...]
                   + q[:, Q_NOPE_W + Q_ROPE_W:] * sq_ref[...]).astype(BF16)
    ckv = _rms_norm(rest[:, C_KV + o:C_KR + o], gkva_ref[...])
    ckv_ref[...] = ckv
    kcb_ref[...] = ckv.astype(BF16)
    kr = rest[:, C_KR + o:C_KRS + o] * ck_ref[...] + rest[:, C_KRS + o:C_KRS + QK_ROPE + o] * sk_ref[...]
    krot_ref[...] = kr
    krb_ref[...] = kr.astype(BF16)
    mq_ref[...] = rest[:, C_MQ + o:C_END + o].astype(BF16)


def _inproj(x2d, state, consts, tables, *, nseg, seg_len, tiles_per_seq, nseq):
    m, d = x2d.shape
    tm = nseg * seg_len
    nt = m // tm
    has_state = state is not None
    ck, sk, cq, sq = tables
    tbl_tiles = ck.shape[0] // tm
    in_specs = [pl.BlockSpec((tm, d), lambda i: (i, 0))]
    args = [x2d]
    if has_state:
        in_specs.append(pl.BlockSpec((nseg, HALO, CONV_CH), lambda i: (i, 0, 0)))
        args.append(state)
    in_specs += [_full(c) for c in consts]
    args += list(consts)
    for t in tables:
        in_specs.append(pl.BlockSpec((tm, t.shape[1]), lambda i: (i % tbl_tiles, 0)))
        args.append(t)
    row = lambda w: pl.BlockSpec((tm, w), lambda i: (i, 0))
    if has_state:
        nconv_spec = pl.BlockSpec((nseg, HALO, CONV_CH), lambda i: (i, 0, 0))
    else:
        nconv_spec = pl.BlockSpec((1, HALO, CONV_CH), lambda i: (i // tiles_per_seq, 0, 0))
    sds = jax.ShapeDtypeStruct
    out_shape = (sds((m, CONV_CH), BF16), sds((nseq, HALO, CONV_CH), F32), sds((m, KV_LORA), F32),
                 sds((m, KV_LORA), BF16), sds((m, QK_ROPE), F32), sds((m, QK_ROPE), BF16),
                 sds((m, Q_NOPE_W), BF16), sds((m, Q_ROPE_W), BF16), sds((m, MEM_HEADS * MEM_HEAD_DIM), BF16))
    out_specs = (row(CONV_CH), nconv_spec, row(KV_LORA), row(KV_LORA), row(QK_ROPE), row(QK_ROPE),
                 row(Q_NOPE_W), row(Q_ROPE_W), row(MEM_HEADS * MEM_HEAD_DIM))
    return pl.pallas_call(
        functools.partial(_inproj_kernel, nseg, seg_len, tiles_per_seq, has_state),
        out_shape=out_shape,
        grid=(nt,),
        in_specs=in_specs,
        out_specs=out_specs,
        scratch_shapes=[pltpu.VMEM((nseg, HALO_ROWS + seg_len, CONV_CH), F32)],
        compiler_params=_params(),
        name="inproj_prompt" if not has_state else "inproj_sample",
    )(*args)


def _rep(x, n):
    return x if n == 1 else jnp.concatenate([x] * n, axis=1)


def _flash_step(qlat, qrot, kc, kr, m_scr, l_scr, acc_scr, mask):
    s = _dot_nt(qlat, kc) + _dot_nt(qrot, kr)
    if mask is not None:
        s = jnp.where(mask, s, -1e30)
    tk = s.shape[1]
    m_prev = m_scr[...]
    m_new = jnp.maximum(m_prev, jnp.max(s, axis=1, keepdims=True))
    alpha = jnp.exp(m_prev - m_new)
    if tk % LANES == 0:
        p = jnp.exp(s - _rep(m_new, tk // LANES))
    else:
        p = jnp.exp(s - m_new[:, 0:1])
    l_scr[...] = alpha * l_scr[...] + jnp.sum(p, axis=1, keepdims=True)
    m_scr[...] = m_new
    acc_scr[...] = _rep(alpha, KV_LORA // LANES) * acc_scr[...] + _dot(p.astype(BF16), kc)


def _stack_queries(tq, qn_ref, qr_ref, wuk_ref, qlat, qrot, m_scr, l_scr, acc_scr):
    for h in range(MLA_HEADS):
        qlat[h * tq:(h + 1) * tq, :] = _dot(qn_ref[:, h * QK_NOPE:(h + 1) * QK_NOPE], wuk_ref[h]).astype(BF16)
        qrot[h * tq:(h + 1) * tq, :] = qr_ref[:, h * QK_ROPE:(h + 1) * QK_ROPE]
    m_scr[...] = jnp.full(m_scr.shape, -jnp.inf, F32)
    l_scr[...] = jnp.zeros(l_scr.shape, F32)
    acc_scr[...] = jnp.zeros(acc_scr.shape, F32)


def _finish_mla(tq, l_scr, acc_scr, wuv_ref, o_ref):
    inv = 1.0 / l_scr[...]
    for h in range(MLA_HEADS):
        rows = slice(h * tq, (h + 1) * tq)
        o_lat = (acc_scr[rows, :] * _rep(inv[rows, :], KV_LORA // LANES)).astype(BF16)
        o_ref[:, h * V_HEAD:(h + 1) * V_HEAD] = _dot(o_lat, wuv_ref[h]).astype(BF16)


def _mem_attention(mq_ref, mk_ref, mv_ref, o_ref):
    mk = mk_ref[0].astype(BF16)
    mv = mv_ref[0].astype(BF16)
    for h in range(MEM_HEADS):
        cols = slice(h * MEM_HEAD_DIM, (h + 1) * MEM_HEAD_DIM)
        s = _dot_nt(mq_ref[:, cols], mk[:, cols]) * MEM_SCALE
        p = jnp.exp(s - jnp.max(s, axis=1, keepdims=True))
        inv = 1.0 / jnp.sum(p, axis=1, keepdims=True)
        o_ref[:, cols] = (_dot(p.astype(BF16), mv[:, cols]) * inv).astype(BF16)


def _attn_prompt_kernel(tq, qn_ref, qr_ref, mq_ref, kc_ref, kr_ref, mk_ref, mv_ref, wuk_ref, wuv_ref,
                        omla_ref, omem_ref, qlat, qrot, m_scr, l_scr, acc_scr):
    i = pl.program_id(1)
    _stack_queries(tq, qn_ref, qr_ref, wuk_ref, qlat, qrot, m_scr, l_scr, acc_scr)

    def keys(j):
        k0 = pl.multiple_of(j * tq, tq)
        return kc_ref[pl.ds(k0, tq), :], kr_ref[pl.ds(k0, tq), :]

    def body(j, carry):
        kc, kr = keys(j)
        _flash_step(qlat[...], qrot[...], kc, kr, m_scr, l_scr, acc_scr, None)
        return carry

    lax.fori_loop(0, i, body, 0)
    rows = MLA_HEADS * tq
    shift = CHUNK.bit_length() - 1
    q_chunk = (lax.broadcasted_iota(I32, (rows, tq), 0) & (tq - 1)) >> shift
    k_chunk = lax.broadcasted_iota(I32, (rows, tq), 1) >> shift
    kc, kr = keys(i)
    _flash_step(qlat[...], qrot[...], kc, kr, m_scr, l_scr, acc_scr, k_chunk <= q_chunk)
    _finish_mla(tq, l_scr, acc_scr, wuv_ref, omla_ref)
    _mem_attention(mq_ref, mk_ref, mv_ref, omem_ref)


def _attn_scratch(tq):
    rows = MLA_HEADS * tq
    return [pltpu.VMEM((rows, KV_LORA), BF16), pltpu.VMEM((rows, QK_ROPE), BF16),
            pltpu.VMEM((rows, LANES), F32), pltpu.VMEM((rows, LANES), F32), pltpu.VMEM((rows, KV_LORA), F32)]


def _attn_prompt(qn, qr, mq, kcb, krb, mk, mv, wuk, wuv, *, nseq, seq):
    m = qn.shape[0]
    tq = ATT_TQ
    nq = seq // tq
    qrow = lambda w: pl.BlockSpec((tq, w), lambda n, i: (n * nq + i, 0))
    mem = pl.BlockSpec((1,) + mk.shape[1:], lambda n, i: (n, 0, 0))
    out = jax.ShapeDtypeStruct((m, MLA_HEADS * V_HEAD), BF16)
    return pl.pallas_call(
        functools.partial(_attn_prompt_kernel, tq),
        out_shape=(out, out),
        grid=(nseq, nq),
        in_specs=[qrow(Q_NOPE_W), qrow(Q_ROPE_W), qrow(mq.shape[1]),
                  pl.BlockSpec((seq, KV_LORA), lambda n, i: (n, 0)),
                  pl.BlockSpec((seq, QK_ROPE), lambda n, i: (n, 0)),
                  mem, mem, _full(wuk), _full(wuv)],
        out_specs=(qrow(MLA_HEADS * V_HEAD),) * 2,
        scratch_shapes=_attn_scratch(tq),
        compiler_params=_params(2),
        name="attn_prompt",
    )(qn, qr, mq, kcb, krb, mk, mv, wuk, wuv)


def _attn_sample_kernel(tq, past, qn_ref, qr_ref, mq_ref, pc_ref, pr_ref, kc_ref, kr_ref, mk_ref, mv_ref,
                        wuk_ref, wuv_ref, omla_ref, omem_ref, qlat, qrot, m_scr, l_scr, acc_scr):
    _stack_queries(tq, qn_ref, qr_ref, wuk_ref, qlat, qrot, m_scr, l_scr, acc_scr)
    for c in range(past // SAMPLE_KV):
        ks = slice(c * SAMPLE_KV, (c + 1) * SAMPLE_KV)
        _flash_step(qlat[...], qrot[...], pc_ref[0, ks, :].astype(BF16), pr_ref[0, ks, :].astype(BF16),
                    m_scr, l_scr, acc_scr, None)
    _flash_step(qlat[...], qrot[...], kc_ref[...], kr_ref[...], m_scr, l_scr, acc_scr, None)
    _finish_mla(tq, l_scr, acc_scr, wuv_ref, omla_ref)
    _mem_attention(mq_ref, mk_ref, mv_ref, omem_ref)


def _attn_sample(qn, qr, mq, past_c, past_r, kcb, krb, mk, mv, wuk, wuv, *, nseq, seq):
    m = qn.shape[0]
    past = past_c.shape[1]
    qrow = lambda w: pl.BlockSpec((seq, w), lambda n: (n, 0))
    per_seq = lambda a: pl.BlockSpec((1,) + a.shape[1:], lambda n: (n, 0, 0))
    out = jax.ShapeDtypeStruct((m, MLA_HEADS * V_HEAD), BF16)
    return pl.pallas_call(
        functools.partial(_attn_sample_kernel, seq, past),
        out_shape=(out, out),
        grid=(nseq,),
        in_specs=[qrow(Q_NOPE_W), qrow(Q_ROPE_W), qrow(mq.shape[1]), per_seq(past_c), per_seq(past_r),
                  qrow(KV_LORA), qrow(QK_ROPE), per_seq(mk), per_seq(mv), _full(wuk), _full(wuv)],
        out_specs=(qrow(MLA_HEADS * V_HEAD),) * 2,
        scratch_shapes=_attn_scratch(seq),
        compiler_params=_params(),
        name="attn_sample",
    )(qn, qr, mq, past_c, past_r, kcb, krb, mk, mv, wuk, wuv)


ROUTE_ROWS = 8
R_E1, R_E2, R_RANK1, R_RANK2, R_CW1, R_CW2 = range(6)
ROUTER_ROWS = 64
R_GROUP0 = N_EXPERTS


def _split_bf16(x):
    hi = x.astype(BF16)
    lo = (x - hi.astype(F32)).astype(BF16)
    return hi, lo


def _merge_kernel(x_ref, cc_ref, om_ref, ox_ref, wg_ref, bg_ref, wco_ref, wmo_ref, wxo_ref, wout_ref,
                  g1_ref, b1_ref, wr_ref, br_ref, x1_ref, x1b_ref, route_ref, cnt_ref):
    d = x_ref.shape[1]
    tm = x_ref.shape[0]
    x = x_ref[...]
    xb = x.astype(BF16)
    merged = None
    for b, (br_ref_, w_ref_) in enumerate(((cc_ref, wco_ref), (om_ref, wmo_ref), (ox_ref, wxo_ref))):
        gate = _sigmoid(_dot(xb, wg_ref[:, b * d:(b + 1) * d]) + bg_ref[:, b * d:(b + 1) * d])
        term = gate * _dot(br_ref_[...], w_ref_[...])
        merged = term if merged is None else merged + term
    z = DEEPNORM_ALPHA * x + _dot(merged.astype(BF16), wout_ref[...])
    x1 = _layer_norm(z, g1_ref[...], b1_ref[...])
    x1_ref[...] = x1
    x1b_ref[...] = x1.astype(BF16)

    xh, xl = _split_bf16(x1)
    wh, wl = _split_bf16(wr_ref[...])
    lt = _dot_nt(wh, xh) + _dot_nt(wh, xl) + _dot_nt(wl, xh) + br_ref[...]

    g = [lt[R_GROUP0 + k:R_GROUP0 + k + 1, :] for k in range(N_GROUPS)]
    gmax = jnp.maximum(jnp.maximum(g[0], g[1]), jnp.maximum(g[2], g[3]))
    gden = sum(jnp.exp(gk - gmax) for gk in g)
    g_val = 1.0 / gden
    g_idx = jnp.where(g[0] == gmax, 0.0, jnp.where(g[1] == gmax, 1.0, jnp.where(g[2] == gmax, 2.0, 3.0)))
    e_sel = jnp.zeros((EXPERTS_PER_GROUP, tm), F32)
    for k in range(N_GROUPS):
        e_sel = jnp.where(g_idx == float(k), lt[k * EXPERTS_PER_GROUP:(k + 1) * EXPERTS_PER_GROUP, :], e_sel)
    sub = lax.broadcasted_iota(I32, (EXPERTS_PER_GROUP, tm), 0).astype(F32)
    big = float(EXPERTS_PER_GROUP)
    m1 = jnp.max(e_sel, axis=0, keepdims=True)
    i1 = jnp.min(jnp.where(e_sel == m1, sub, big), axis=0, keepdims=True)
    rest = jnp.where(sub == i1, -jnp.inf, e_sel)
    m2 = jnp.max(rest, axis=0, keepdims=True)
    i2 = jnp.min(jnp.where(rest == m2, sub, big), axis=0, keepdims=True)
    eden = jnp.sum(jnp.exp(e_sel - m1), axis=0, keepdims=True)
    p1 = 1.0 / eden
    p2 = jnp.exp(m2 - m1) / eden
    psum = p1 + p2
    cw1 = g_val * (p1 / psum)
    cw2 = g_val * (p2 / psum)
    e1 = g_idx * big + i1
    e2 = g_idx * big + i2

    eio = lax.broadcasted_iota(I32, (N_EXPERTS, tm), 0).astype(F32)
    oh1 = eio == e1
    oh2 = eio == e2
    onehot = jnp.where(oh1, 1.0, jnp.where(oh2, 1.0, 0.0))
    upper = jnp.where(lax.broadcasted_iota(I32, (tm, tm), 0) <= lax.broadcasted_iota(I32, (tm, tm), 1), 1.0, 0.0)
    prefix = _dot(onehot.astype(BF16), upper.astype(BF16))
    rank1 = jnp.sum(jnp.where(oh1, prefix, 0.0), axis=0, keepdims=True) - 1.0
    rank2 = jnp.sum(jnp.where(oh2, prefix, 0.0), axis=0, keepdims=True) - 1.0
    route_ref[R_E1:R_E1 + 1, :] = e1
    route_ref[R_E2:R_E2 + 1, :] = e2
    route_ref[R_RANK1:R_RANK1 + 1, :] = rank1
    route_ref[R_RANK2:R_RANK2 + 1, :] = rank2
    route_ref[R_CW1:R_CW1 + 1, :] = cw1
    route_ref[R_CW2:R_CW2 + 1, :] = cw2
    route_ref[R_CW2 + 1:ROUTE_ROWS, :] = jnp.zeros((ROUTE_ROWS - R_CW2 - 1, tm), F32)
    cnt_ref[0] = jnp.broadcast_to(jnp.sum(onehot, axis=1, keepdims=True), (N_EXPERTS, LANES))


def _merge(x2d, cconv, omla, omem, consts):
    m, d = x2d.shape
    tm = TOKEN_TILE
    nt = m // tm
    row = lambda w: pl.BlockSpec((tm, w), lambda i: (i, 0))
    sds = jax.ShapeDtypeStruct
    return pl.pallas_call(
        _merge_kernel,
        out_shape=(sds((m, d), F32), sds((m, d), BF16), sds((ROUTE_ROWS, m), F32),
                   sds((nt, N_EXPERTS, LANES), F32)),
        grid=(nt,),
        in_specs=[row(d), row(cconv.shape[1]), row(omla.shape[1]), row(omem.shape[1])] + [_full(c) for c in consts],
        out_specs=(row(d), row(d), pl.BlockSpec((ROUTE_ROWS, tm), lambda i: (0, i)),
                   pl.BlockSpec((1, N_EXPERTS, LANES), lambda i: (i, 0, 0))),
        compiler_params=_params(),
        name="merge",
    )(x2d, cconv, omla, omem, *consts)


def _local_positions(route_ref, lsb_ref):
    tm = route_ref.shape[1]
    lsb = lsb_ref[0]
    eio = lax.broadcasted_iota(I32, (N_EXPERTS, tm), 0).astype(F32)
    pos = []
    for r_e, r_rank in ((R_E1, R_RANK1), (R_E2, R_RANK2)):
        e = route_ref[r_e:r_e + 1, :]
        start = jnp.sum(jnp.where(eio == e, lsb, 0.0), axis=0, keepdims=True)
        pos.append((start + route_ref[r_rank:r_rank + 1, :]).astype(I32))
    return pos


def _segment_copies(i, ls_ref, nch_ref, gd_ref, make_copy):
    def seg(e, carry):
        k = i * N_EXPERTS + e
        ls = ls_ref[k]
        gd = gd_ref[k]

        def chunk(c, carry2):
            make_copy(pl.multiple_of(ls + c * ROW_CHUNK, ROW_CHUNK),
                      pl.multiple_of(gd + c * ROW_CHUNK, ROW_CHUNK)).start()
            return carry2

        return lax.fori_loop(0, nch_ref[k], chunk, carry)

    lax.fori_loop(0, N_EXPERTS, seg, 0)


def _dispatch_kernel(ls_ref, nch_ref, gd_ref, ntot_ref, x_ref, route_ref, lsb_ref, buf_in_ref, buf_ref,
                     xs_scr, sem):
    del buf_in_ref
    i = pl.program_id(0)
    tm = x_ref.shape[0]
    lp1, lp2 = _local_positions(route_ref, lsb_ref)
    rowi = lax.broadcasted_iota(I32, (SORT_ROWS, tm), 0)
    sel1 = rowi == lp1
    sel2 = rowi == lp2
    perm = jnp.where(sel1, 1.0, jnp.where(sel2, 1.0, 0.0)).astype(BF16)
    xs = _dot(perm, x_ref[...])
    cw = jnp.sum(jnp.where(sel1, route_ref[R_CW1:R_CW1 + 1, :],
                           jnp.where(sel2, route_ref[R_CW2:R_CW2 + 1, :], 0.0)), axis=1, keepdims=True)
    lo = lax.bitcast_convert_type(xs[:, :PACKED], U32) >> 16
    hi = lax.bitcast_convert_type(xs[:, PACKED:], U32) & jnp.uint32(0xFFFF0000)
    xs_scr[:, 0:PACKED] = hi | lo
    xs_scr[:, PACKED:PACKED + LANES] = lax.bitcast_convert_type(jnp.broadcast_to(cw, (SORT_ROWS, LANES)), U32)

    def make_copy(src_row, dst_row):
        return pltpu.make_async_copy(xs_scr.at[pl.ds(src_row, ROW_CHUNK)], buf_ref.at[pl.ds(dst_row, ROW_CHUNK)], sem)

    _segment_copies(i, ls_ref, nch_ref, gd_ref, make_copy)

    def drain(c, carry):
        make_copy(0, 0).wait()
        return carry

    lax.fori_loop(0, ntot_ref[i], drain, 0)


def _dispatch(seg, x1b, route, lsb, buf):
    m, d = x1b.shape
    tm = TOKEN_TILE
    nt = m // tm
    width = buf.shape[1]
    grid_spec = pltpu.PrefetchScalarGridSpec(
        num_scalar_prefetch=4,
        grid=(nt,),
        in_specs=[pl.BlockSpec((tm, d), lambda i, *_: (i, 0)),
                  pl.BlockSpec((ROUTE_ROWS, tm), lambda i, *_: (0, i)),
                  pl.BlockSpec((1, N_EXPERTS, tm), lambda i, *_: (i, 0, 0)),
                  pl.BlockSpec(memory_space=pl.ANY)],
        out_specs=pl.BlockSpec(memory_space=pl.ANY),
        scratch_shapes=[pltpu.VMEM((SORT_ROWS, width), U32), pltpu.SemaphoreType.DMA(())],
    )
    return pl.pallas_call(
        _dispatch_kernel,
        out_shape=jax.ShapeDtypeStruct(buf.shape, buf.dtype),
        grid_spec=grid_spec,
        input_output_aliases={7: 0},
        compiler_params=_params(),
        name="dispatch",
    )(*seg, x1b, route, lsb, buf)


def _unpack(u):
    lo = lax.bitcast_convert_type(u << 16, F32).astype(BF16)
    hi = lax.bitcast_convert_type(u & jnp.uint32(0xFFFF0000), F32).astype(BF16)
    return lo, hi


def _experts_kernel(te_ref, nu_ref, xs_ref, w1_ref, w3_ref, w2_ref, y_ref):
    @pl.when(pl.program_id(0) >= nu_ref[0])
    def _():
        y_ref[...] = jnp.zeros(y_ref.shape, U32)

    @pl.when(pl.program_id(0) < nu_ref[0])
    def _():
        u = xs_ref[:, 0:PACKED]
        cw = lax.bitcast_convert_type(xs_ref[:, PACKED:PACKED + LANES], F32)[:, 0:1]
        xa, xb = _unpack(u)
        w1 = w1_ref[0].astype(BF16)
        w3 = w3_ref[0].astype(BF16)
        h1 = _dot(xa, w1[:PACKED]) + _dot(xb, w1[PACKED:])
        h3 = _dot(xa, w3[:PACKED]) + _dot(xb, w3[PACKED:])
        h = (h1 * _sigmoid(h1) * h3 * cw).astype(BF16)
        y = _dot(h, w2_ref[0].astype(BF16)).astype(BF16).astype(F32)
        lo = lax.bitcast_convert_type(y[:, :PACKED], U32) >> 16
        hi = lax.bitcast_convert_type(y[:, PACKED:], U32) & jnp.uint32(0xFFFF0000)
        y_ref[...] = hi | lo


def _experts(te, nu, buf, w1, w3, w2):
    rows, width = buf.shape
    tr = EXPERT_TILE
    clamp = lambda i, te_ref, nu_ref: (jnp.minimum(i, nu_ref[0] - 1), 0)
    wspec = lambda w: pl.BlockSpec((1,) + w.shape[1:], lambda i, te_ref, nu_ref: (te_ref[i], 0, 0))
    grid_spec = pltpu.PrefetchScalarGridSpec(
        num_scalar_prefetch=2,
        grid=(rows // tr,),
        in_specs=[pl.BlockSpec((tr, width), clamp), wspec(w1), wspec(w3), wspec(w2)],
        out_specs=pl.BlockSpec((tr, PACKED), lambda i, te_ref, nu_ref: (i, 0)),
    )
    return pl.pallas_call(
        _experts_kernel,
        out_shape=jax.ShapeDtypeStruct((rows, PACKED), U32),
        grid_spec=grid_spec,
        compiler_params=_params(),
        name="experts",
    )(te, nu, buf, w1, w3, w2)


def _combine_kernel(ls_ref, nch_ref, gd_ref, ntot_ref, y_hbm, route_ref, lsb_ref, x1_ref, g2_ref, b2_ref,
                    out_ref, ys_scr, sem):
    i = pl.program_id(0)
    tm = x1_ref.shape[0]
    ys_scr[...] = jnp.zeros(ys_scr.shape, U32)

    def make_copy(src_row, dst_row):
        return pltpu.make_async_copy(y_hbm.at[pl.ds(dst_row, ROW_CHUNK)], ys_scr.at[pl.ds(src_row, ROW_CHUNK)], sem)

    _segment_copies(i, ls_ref, nch_ref, gd_ref, make_copy)
    lp1, lp2 = _local_positions(route_ref, lsb_ref)
    rowi = lax.broadcasted_iota(I32, (SORT_ROWS, tm), 0)
    perm = jnp.where(rowi == lp1, 1.0, jnp.where(rowi == lp2, 1.0, 0.0)).astype(BF16)

    def drain(c, carry):
        make_copy(0, 0).wait()
        return carry

    lax.fori_loop(0, ntot_ref[i], drain, 0)
    ya, yb = _unpack(ys_scr[...])
    y = jnp.concatenate([_dot_tn(perm, ya), _dot_tn(perm, yb)], axis=1)
    out_ref[...] = _layer_norm(DEEPNORM_ALPHA * x1_ref[...] + y, g2_ref[...], b2_ref[...])


def _combine(seg, ybuf, route, lsb, x1, g2, b2):
    m, d = x1.shape
    tm = TOKEN_TILE
    nt = m // tm
    grid_spec = pltpu.PrefetchScalarGridSpec(
        num_scalar_prefetch=4,
        grid=(nt,),
        in_specs=[pl.BlockSpec(memory_space=pl.ANY),
                  pl.BlockSpec((ROUTE_ROWS, tm), lambda i, *_: (0, i)),
                  pl.BlockSpec((1, N_EXPERTS, tm), lambda i, *_: (i, 0, 0)),
                  pl.BlockSpec((tm, d), lambda i, *_: (i, 0)),
                  pl.BlockSpec(g2.shape, lambda i, *_: (0, 0)),
                  pl.BlockSpec(b2.shape, lambda i, *_: (0, 0))],
        out_specs=pl.BlockSpec((tm, d), lambda i, *_: (i, 0)),
        scratch_shapes=[pltpu.VMEM((SORT_ROWS, PACKED), U32), pltpu.SemaphoreType.DMA(())],
    )
    return pl.pallas_call(
        _combine_kernel,
        out_shape=jax.ShapeDtypeStruct((m, d), F32),
        grid_spec=grid_spec,
        compiler_params=_params(),
        name="combine",
    )(*seg, ybuf, route, lsb, x1, g2, b2)


def _rope_tables(pos):
    half = QK_ROPE // 2
    inv = jnp.power(ROPE_BASE, -jnp.arange(half, dtype=F32) / half)
    ang = pos.astype(F32)[:, None] * inv[None, :]
    cos, sin = jnp.cos(ang), jnp.sin(ang)
    ck = jnp.concatenate([cos, cos], axis=1)
    sk = jnp.concatenate([-sin, sin], axis=1)
    cq = jnp.tile(ck, (1, MLA_HEADS)) * MLA_SCALE
    sq = jnp.tile(sk, (1, MLA_HEADS)) * MLA_SCALE
    return ck, sk, cq, sq


def _ceil_to(x, k):
    return (x + k - 1) // k * k


def _segment_tables(cnt, n_row_tiles):
    cnt8 = _ceil_to(cnt, ROW_CHUNK)
    lstart = jnp.cumsum(cnt8, axis=1) - cnt8
    tot = _ceil_to(jnp.sum(cnt8, axis=0), EXPERT_TILE)
    ends = jnp.cumsum(tot)
    gdst = (ends - tot)[None, :] + jnp.cumsum(cnt8, axis=0) - cnt8
    nch = cnt8 // ROW_CHUNK
    n_used = ends[-1] // EXPERT_TILE
    tile_row = jnp.minimum(jnp.arange(n_row_tiles, dtype=I32), n_used - 1) * EXPERT_TILE
    tile_expert = jnp.minimum(jnp.searchsorted(ends, tile_row, side="right"), N_EXPERTS - 1).astype(I32)
    return lstart, gdst, nch, jnp.sum(nch, axis=1), tile_expert, n_used.reshape(1).astype(I32)


def kernel(x_prompt, x_sample, cache_conv, cache_mla_latent, cache_mla_krope, cache_mem_k, cache_mem_v,
           mem_prompt, w_in, b_gate, w_dw, b_dw, g_cn, b_cn, w_co, g_qa, w_uq, g_kva, w_uk, w_uv, w_mo,
           w_mem_k, w_mem_v, w_mem_o, w_out, g_ln1, b_ln1, w_group, b_group, w_router, b_router,
           w_e1, w_e3, w_e2, g_ln2, b_ln2):
    n_p, s_p, d = x_prompt.shape
    n_s, s_s, _ = x_sample.shape
    past = cache_mla_latent.shape[2]
    n_mem = mem_prompt.shape[1]
    assert w_in.shape[0] == DEPTH == 1
    assert s_p % TOKEN_TILE == 0 and s_p % ATT_TQ == 0 and TOKEN_TILE % s_s == 0 and (n_s * s_s) % TOKEN_TILE == 0
    assert s_s <= CHUNK and past % CHUNK == 0 and past % SAMPLE_KV == 0
    mem_w = MEM_HEADS * MEM_HEAD_DIM
    row2 = lambda v: v.reshape(1, -1)

    wl = w_in[0]
    sp = (C_GLU, C_GLU + Q_LORA, C_GLU + Q_LORA + KV_LORA + QK_ROPE, C_GLU + Q_LORA + KV_LORA + QK_ROPE + mem_w)
    w_glu, w_qd, w_kvd, w_mq, w_gate = wl[:, :sp[0]], wl[:, sp[0]:sp[1]], wl[:, sp[1]:sp[2]], wl[:, sp[2]:sp[3]], wl[:, sp[3]:]
    half = QK_ROPE // 2
    w_kr = w_kvd[:, KV_LORA:]
    w_krs = jnp.concatenate([w_kr[:, half:], w_kr[:, :half]], axis=1)
    pad = jnp.zeros((d, C_MQ - (C_KRS + QK_ROPE)), F32)
    w_a = jnp.concatenate([w_glu, w_qd, w_kvd, w_krs, pad, w_mq], axis=1).astype(BF16)
    wq = w_uq[0].reshape(Q_LORA, MLA_HEADS, QK_NOPE + QK_ROPE)
    wq_r = wq[:, :, QK_NOPE:]
    wq_rs = jnp.concatenate([wq_r[:, :, half:], wq_r[:, :, :half]], axis=2)
    w_uq2 = jnp.concatenate([wq[:, :, :QK_NOPE].reshape(Q_LORA, -1), wq_r.reshape(Q_LORA, -1),
                             wq_rs.reshape(Q_LORA, -1)], axis=1).astype(BF16)
    wdw8 = jnp.broadcast_to(w_dw[0][:, None, :], (CONV_WIDTH, SUBLANES, CONV_CH))
    in_consts = (w_a, w_uq2, wdw8, row2(b_dw[0]), row2(g_cn[0]), row2(b_cn[0]), row2(g_qa[0]), row2(g_kva[0]))
    wuk_t = jnp.transpose(w_uk[0], (1, 2, 0)).astype(BF16)
    wuv_h = jnp.transpose(w_uv[0], (1, 0, 2)).astype(BF16)
    w_r = jnp.zeros((ROUTER_ROWS, d), F32).at[:N_EXPERTS].set(w_router[0].T).at[R_GROUP0:R_GROUP0 + N_GROUPS].set(w_group[0].T)
    b_r = jnp.zeros((ROUTER_ROWS, 1), F32).at[:N_EXPERTS, 0].set(b_router[0]).at[R_GROUP0:R_GROUP0 + N_GROUPS, 0].set(b_group[0])
    merge_consts = (w_gate.astype(BF16), row2(b_gate[0]), w_co[0].astype(BF16), w_mo[0].astype(BF16),
                    w_mem_o[0].astype(BF16), w_out[0].astype(BF16), row2(g_ln1[0]), row2(b_ln1[0]), w_r, b_r)

    xp = x_prompt.reshape(n_p * s_p, d)
    mk_p, mv_p = _mem_proj(mem_prompt, w_mem_k[0].astype(BF16), w_mem_v[0].astype(BF16))
    tabs_p = _rope_tables(jnp.arange(s_p, dtype=I32))
    (cc_p, nconv_p, ckv_p, kcb_p, krot_p, krb_p, qn_p, qr_p, mq_p) = _inproj(
        xp, None, in_consts, tabs_p, nseg=1, seg_len=TOKEN_TILE, tiles_per_seq=s_p // TOKEN_TILE, nseq=n_p)
    omla_p, omem_p = _attn_prompt(qn_p, qr_p, mq_p, kcb_p, krb_p, mk_p, mv_p, wuk_t, wuv_h, nseq=n_p, seq=s_p)
    x1_p, x1b_p, route_p, cnt_p = _merge(xp, cc_p, omla_p, omem_p, merge_consts)

    xs = x_sample.reshape(n_s * s_s, d)
    seqs_per_tile = TOKEN_TILE // s_s
    tabs_s = tuple(jnp.tile(t, (seqs_per_tile, 1)) for t in _rope_tables(past + jnp.arange(s_s, dtype=I32)))
    (cc_s, nconv_s, ckv_s, kcb_s, krot_s, krb_s, qn_s, qr_s, mq_s) = _inproj(
        xs, cache_conv[0], in_consts, tabs_s, nseg=seqs_per_tile, seg_len=s_s, tiles_per_seq=1, nseq=n_s)
    omla_s, omem_s = _attn_sample(qn_s, qr_s, mq_s, cache_mla_latent[0], cache_mla_krope[0], kcb_s, krb_s,
                                  cache_mem_k[0].reshape(n_s, n_mem, mem_w), cache_mem_v[0].reshape(n_s, n_mem, mem_w),
                                  wuk_t, wuv_h, nseq=n_s, seq=s_s)
    x1_s, x1b_s, route_s, cnt_s = _merge(xs, cc_s, omla_s, omem_s, merge_consts)

    nt_p, nt_s = cnt_p.shape[0], cnt_s.shape[0]
    cnt = jnp.concatenate([cnt_p[:, :, 0], cnt_s[:, :, 0]], axis=0).astype(I32)
    m_tot = (nt_p + nt_s) * TOKEN_TILE
    buf_rows = _ceil_to(2 * m_tot + (nt_p + nt_s) * N_EXPERTS * (ROW_CHUNK - 1) + N_EXPERTS * (EXPERT_TILE - 1), EXPERT_TILE)
    lstart, gdst, nch, ntot, tile_expert, n_used = _segment_tables(cnt, buf_rows // EXPERT_TILE)
    lsb = jnp.broadcast_to(lstart.astype(F32)[:, :, None], lstart.shape + (TOKEN_TILE,))
    seg = lambda a, b: (lstart[a:b].reshape(-1), nch[a:b].reshape(-1), gdst[a:b].reshape(-1), ntot[a:b])
    seg_p, seg_s = seg(0, nt_p), seg(nt_p, nt_p + nt_s)
    buf = jnp.zeros((buf_rows, PACKED + LANES), U32)
    buf = _dispatch(seg_p, x1b_p, route_p, lsb[:nt_p], buf)
    buf = _dispatch(seg_s, x1b_s, route_s, lsb[nt_p:], buf)
    n_e = N_EXPERTS
    ybuf = _experts(tile_expert, n_used, buf, w_e1[0].reshape((n_e,) + w_e1.shape[3:]),
                    w_e3[0].reshape((n_e,) + w_e3.shape[3:]), w_e2[0].reshape((n_e,) + w_e2.shape[3:]))
    g2, b2 = row2(g_ln2[0]), row2(b_ln2[0])
    y_p = _combine(seg_p, ybuf, route_p, lsb[:nt_p], x1_p, g2, b2)
    y_s = _combine(seg_s, ybuf, route_s, lsb[nt_p:], x1_s, g2, b2)

    lead = lambda a, n, t: a.reshape((1, n, t) + a.shape[1:])
    return (y_p.reshape(n_p, s_p, d), y_s.reshape(n_s, s_s, d),
            nconv_p[None], lead(ckv_p, n_p, s_p), lead(krot_p, n_p, s_p),
            mk_p.reshape(1, n_p, n_mem, MEM_HEADS, MEM_HEAD_DIM), mv_p.reshape(1, n_p, n_mem, MEM_HEADS, MEM_HEAD_DIM),
            nconv_s[None], lead(ckv_s, n_s, s_s), lead(krot_s, n_s, s_s))
```

```python
import functools

import jax
import jax.numpy as jnp
from jax import lax
from jax.experimental import pallas as pl
from jax.experimental.pallas import tpu as pltpu

F32 = jnp.float32
BF16 = jnp.bfloat16
U32 = jnp.uint32
I32 = jnp.int32

DEPTH = 1
CHUNK = 64
EPS = 1e-5
CONV_CH = 512
CONV_WIDTH = 31
HALO = CONV_WIDTH - 1
MLA_HEADS = 8
Q_LORA = 384
KV_LORA = 256
QK_NOPE = 64
QK_ROPE = 32
V_HEAD = 64
ROPE_BASE = 10000.0
MLA_SCALE = (QK_NOPE + QK_ROPE) ** -0.5
MEM_HEADS = 4
MEM_HEAD_DIM = 128
MEM_SCALE = MEM_HEAD_DIM ** -0.5
N_GROUPS = 4
EXPERTS_PER_GROUP = 8
N_EXPERTS = N_GROUPS * EXPERTS_PER_GROUP
DEEPNORM_ALPHA = (2.0 * DEPTH) ** 0.25

SUBLANES = 8
LANES = 128
TOKEN_TILE = 512
HALO_ROWS = 32
CONV_ROWS = 32
ATT_TQ = 256
SAMPLE_KV = 512
ROW_CHUNK = SUBLANES
EXPERT_TILE = 512
ZERO_MID = 64
SORT_ROWS = 2 * TOKEN_TILE + N_EXPERTS * ROW_CHUNK
CW_LANES = LANES
VMEM_LIMIT = 56 * 1024 * 1024


def _dot(a, b):
    return jnp.dot(a, b, preferred_element_type=F32)


def _dot_nt(a, b):
    return lax.dot_general(a, b, (((1,), (1,)), ((), ())), preferred_element_type=F32)


def _dot_tn(a, b):
    return lax.dot_general(a, b, (((0,), (0,)), ((), ())), preferred_element_type=F32)


def _sigmoid(x):
    return 1.0 / (1.0 + jnp.exp(-x))


def _layer_norm(x, g, b):
    mu = jnp.mean(x, axis=-1, keepdims=True)
    d = x - mu
    var = jnp.mean(d * d, axis=-1, keepdims=True)
    return d * lax.rsqrt(var + EPS) * g + b


def _rms_norm(x, g):
    return x * lax.rsqrt(jnp.mean(x * x, axis=-1, keepdims=True) + EPS) * g


def _full(a):
    nd = a.ndim
    return pl.BlockSpec(a.shape, lambda *_: (0,) * nd)


def _params(n_axes=1, vmem=VMEM_LIMIT):
    return pltpu.CompilerParams(dimension_semantics=("arbitrary",) * n_axes, vmem_limit_bytes=vmem)


def _head_rows(h, n_tok):
    return pl.ds(h, n_tok, stride=MEM_HEADS)


def _mem_proj_kernel(m_ref, wk_ref, wv_ref, k_ref, v_ref):
    m = m_ref[0].astype(BF16)
    t = m.shape[0]
    k = _dot(m, wk_ref[...])
    v = _dot(m, wv_ref[...])
    for h in range(MEM_HEADS):
        cols = slice(h * MEM_HEAD_DIM, (h + 1) * MEM_HEAD_DIM)
        k_ref.at[0][_head_rows(h, t), :] = k[:, cols]
        v_ref.at[0][_head_rows(h, t), :] = v[:, cols]


def _mem_proj(mem, wk, wv):
    n, t, d = mem.shape
    out = jax.ShapeDtypeStruct((n, t * MEM_HEADS, MEM_HEAD_DIM), F32)
    return pl.pallas_call(
        _mem_proj_kernel,
        out_shape=(out, out),
        grid=(n,),
        in_specs=[pl.BlockSpec((1, t, d), lambda i: (i, 0, 0)), _full(wk), _full(wv)],
        out_specs=(pl.BlockSpec((1, t * MEM_HEADS, MEM_HEAD_DIM), lambda i: (i, 0, 0)),) * 2,
        compiler_params=_params(),
        name="mem_proj",
    )(mem, wk, wv)


C_GLU = 2 * CONV_CH
C_QD = C_GLU
C_KV = C_QD + Q_LORA
C_KR = C_KV + KV_LORA
C_KRS = C_KR + QK_ROPE
C_MQ = 1792
C_END = C_MQ + MEM_HEADS * MEM_HEAD_DIM
Q_NOPE_W = MLA_HEADS * QK_NOPE
Q_ROPE_W = MLA_HEADS * QK_ROPE


def _inproj_kernel(nseg, seg_len, tiles_per_seq, has_state, *refs):
    it = iter(refs)
    x_ref = next(it)
    st_ref = next(it) if has_state else None
    (w_ref, wuq_ref, wdw_ref, bdw_ref, gcn_ref, bcn_ref, gqa_ref, gkva_ref,
     ck_ref, sk_ref, cq_ref, sq_ref,
     cconv_ref, nconv_ref, ckv_ref, kcb_ref, krot_ref, krb_ref, qn_ref, qr_ref, mq_ref,
     win) = it

    xb = x_ref[...].astype(BF16)
    ab = _dot(xb, w_ref[:, 0:C_GLU])
    u = ab[:, :CONV_CH] * _sigmoid(ab[:, CONV_CH:])

    if has_state:
        for s in range(nseg):
            win[s, HALO_ROWS - HALO:HALO_ROWS, :] = st_ref[s]
            win[s, HALO_ROWS:HALO_ROWS + seg_len, :] = u[s * seg_len:(s + 1) * seg_len]
    else:
        j = pl.program_id(0) % tiles_per_seq

        @pl.when(j == 0)
        def _():
            win[0, 0:HALO_ROWS, :] = jnp.zeros((HALO_ROWS, CONV_CH), F32)

        @pl.when(j != 0)
        def _():
            win[0, 0:HALO_ROWS, :] = win[0, seg_len:seg_len + HALO_ROWS, :]

        win[0, HALO_ROWS:HALO_ROWS + seg_len, :] = u

    g_cn = gcn_ref[...]
    b_cn = bcn_ref[...]
    b_dw = bdw_ref[...]
    base = HALO_ROWS - HALO
    span = HALO_ROWS + CONV_ROWS
    for s in range(nseg):
        for rc in range(seg_len // CONV_ROWS):
            r0 = rc * CONV_ROWS
            xw = win[s, r0:r0 + span, :]
            acc = jnp.zeros((CONV_ROWS // SUBLANES, SUBLANES, CONV_CH), F32) + b_dw
            for b in range(SUBLANES):
                xsh = xw if b == 0 else xw[b:b + span - SUBLANES, :]
                for a in range(span // SUBLANES):
                    k = a * SUBLANES + b - base
                    if 0 <= k < CONV_WIDTH:
                        xk = xsh[a * SUBLANES:a * SUBLANES + CONV_ROWS, :]
                        acc = acc + xk.reshape(CONV_ROWS // SUBLANES, SUBLANES, CONV_CH) * wdw_ref[k]
            y = _layer_norm(acc.reshape(CONV_ROWS, CONV_CH), g_cn, b_cn)
            y = y * _sigmoid(y)
            cconv_ref[s * seg_len + r0:s * seg_len + r0 + CONV_ROWS, :] = y.astype(BF16)
        nconv_ref[s] = win[s, seg_len + HALO_ROWS - HALO:seg_len + HALO_ROWS, :]

    rest = _dot(xb, w_ref[:, C_QD:C_END])
    o = -C_QD
    qd = rest[:, C_QD + o:C_KV + o]
    qn = _rms_norm(qd, gqa_ref[...]).astype(BF16)
    q = _dot(qn, wuq_ref[...])
    qn_ref[...] = (q[:, :Q_NOPE_W] * MLA_SCALE).astype(BF16)
    qr_ref[...] = (q[:, Q_NOPE_W:Q_NOPE_W + Q_ROPE_W] * cq_ref[...]
                   + q[:, Q_NOPE_W + Q_ROPE_W:] * sq_ref[...]).astype(BF16)
    ckv = _rms_norm(rest[:, C_KV + o:C_KR + o], gkva_ref[...])
    ckv_ref[...] = ckv
    kcb_ref[...] = ckv.astype(BF16)
    kr = rest[:, C_KR + o:C_KRS + o] * ck_ref[...] + rest[:, C_KRS + o:C_KRS + QK_ROPE + o] * sk_ref[...]
    krot_ref[...] = kr
    krb_ref[...] = kr.astype(BF16)
    mq_ref[...] = rest[:, C_MQ + o:C_END + o].astype(BF16)


def _inproj(x2d, state, consts, tables, *, nseg, seg_len, tiles_per_seq, nseq):
    m, d = x2d.shape
    tm = nseg * seg_len
    nt = m // tm
    has_state = state is not None
    ck, sk, cq, sq = tables
    tbl_tiles = ck.shape[0] // tm
    in_specs = [pl.BlockSpec((tm, d), lambda i: (i, 0))]
    args = [x2d]
    if has_state:
        in_specs.append(pl.BlockSpec((nseg, HALO, CONV_CH), lambda i: (i, 0, 0)))
        args.append(state)
    in_specs += [_full(c) for c in consts]
    args += list(consts)
    for t in tables:
        in_specs.append(pl.BlockSpec((tm, t.shape[1]), lambda i: (i % tbl_tiles, 0)))
        args.append(t)
    row = lambda w: pl.BlockSpec((tm, w), lambda i: (i, 0))
    if has_state:
        nconv_spec = pl.BlockSpec((nseg, HALO, CONV_CH), lambda i: (i, 0, 0))
    else:
        nconv_spec = pl.BlockSpec((1, HALO, CONV_CH), lambda i: (i // tiles_per_seq, 0, 0))
    sds = jax.ShapeDtypeStruct
    out_shape = (sds((m, CONV_CH), BF16), sds((nseq, HALO, CONV_CH), F32), sds((m, KV_LORA), F32),
                 sds((m, KV_LORA), BF16), sds((m, QK_ROPE), F32), sds((m, QK_ROPE), BF16),
                 sds((m, Q_NOPE_W), BF16), sds((m, Q_ROPE_W), BF16), sds((m, MEM_HEADS * MEM_HEAD_DIM), BF16))
    out_specs = (row(CONV_CH), nconv_spec, row(KV_LORA), row(KV_LORA), row(QK_ROPE), row(QK_ROPE),
                 row(Q_NOPE_W), row(Q_ROPE_W), row(MEM_HEADS * MEM_HEAD_DIM))
    return pl.pallas_call(
        functools.partial(_inproj_kernel, nseg, seg_len, tiles_per_seq, has_state),
        out_shape=out_shape,
        grid=(nt,),
        in_specs=in_specs,
        out_specs=out_specs,
        scratch_shapes=[pltpu.VMEM((nseg, HALO_ROWS + seg_len, CONV_CH), F32)],
        compiler_params=_params(),
        name="inproj_prompt" if not has_state else "inproj_sample",
    )(*args)


def _rep(x, n):
    return x if n == 1 else jnp.concatenate([x] * n, axis=1)


def _flash_step(qlat, qrot, kc, kr, m_scr, l_scr, acc_scr, mask):
    s = _dot_nt(qlat, kc) + _dot_nt(qrot, kr)
    if mask is not None:
        s = jnp.where(mask, s, -1e30)
    tk = s.shape[1]
    m_prev = m_scr[...]
    m_new = jnp.maximum(m_prev, jnp.max(s, axis=1, keepdims=True))
    alpha = jnp.exp(m_prev - m_new)
    if tk % LANES == 0:
        p = jnp.exp(s - _rep(m_new, tk // LANES))
    else:
        p = jnp.exp(s - m_new[:, 0:1])
    l_scr[...] = alpha * l_scr[...] + jnp.sum(p, axis=1, keepdims=True)
    m_scr[...] = m_new
    acc_scr[...] = _rep(alpha, KV_LORA // LANES) * acc_scr[...] + _dot(p.astype(BF16), kc)


def _stack_queries(tq, qn_ref, qr_ref, wuk_ref, qlat, qrot, m_scr, l_scr, acc_scr):
    for h in range(MLA_HEADS):
        qlat[h * tq:(h + 1) * tq, :] = _dot(qn_ref[:, h * QK_NOPE:(h + 1) * QK_NOPE], wuk_ref[h]).astype(BF16)
        qrot[h * tq:(h + 1) * tq, :] = qr_ref[:, h * QK_ROPE:(h + 1) * QK_ROPE]
    m_scr[...] = jnp.full(m_scr.shape, -jnp.inf, F32)
    l_scr[...] = jnp.zeros(l_scr.shape, F32)
    acc_scr[...] = jnp.zeros(acc_scr.shape, F32)


def _finish_mla(tq, l_scr, acc_scr, wuv_ref, o_ref):
    inv = 1.0 / l_scr[...]
    for h in range(MLA_HEADS):
        rows = slice(h * tq, (h + 1) * tq)
        o_lat = (acc_scr[rows, :] * _rep(inv[rows, :], KV_LORA // LANES)).astype(BF16)
        o_ref[:, h * V_HEAD:(h + 1) * V_HEAD] = _dot(o_lat, wuv_ref[h]).astype(BF16)


def _mem_attention(mq_ref, mk_ref, mv_ref, o_ref):
    n_tok = mk_ref.shape[1] // MEM_HEADS
    for h in range(MEM_HEADS):
        cols = slice(h * MEM_HEAD_DIM, (h + 1) * MEM_HEAD_DIM)
        mk = mk_ref.at[0][_head_rows(h, n_tok), :].astype(BF16)
        mv = mv_ref.at[0][_head_rows(h, n_tok), :].astype(BF16)
        s = _dot_nt(mq_ref[:, cols], mk) * MEM_SCALE
        p = jnp.exp(s - jnp.max(s, axis=1, keepdims=True))
        inv = 1.0 / jnp.sum(p, axis=1, keepdims=True)
        o_ref[:, cols] = (_dot(p.astype(BF16), mv) * inv).astype(BF16)


def _attn_prompt_kernel(tq, qn_ref, qr_ref, mq_ref, kc_ref, kr_ref, mk_ref, mv_ref, wuk_ref, wuv_ref,
                        omla_ref, omem_ref, qlat, qrot, m_scr, l_scr, acc_scr):
    i = pl.program_id(1)
    _stack_queries(tq, qn_ref, qr_ref, wuk_ref, qlat, qrot, m_scr, l_scr, acc_scr)

    def keys(j):
        k0 = pl.multiple_of(j * tq, tq)
        return kc_ref[pl.ds(k0, tq), :], kr_ref[pl.ds(k0, tq), :]

    def body(j, carry):
        kc, kr = keys(j)
        _flash_step(qlat[...], qrot[...], kc, kr, m_scr, l_scr, acc_scr, None)
        return carry

    lax.fori_loop(0, i, body, 0)
    rows = MLA_HEADS * tq
    shift = CHUNK.bit_length() - 1
    q_chunk = (lax.broadcasted_iota(I32, (rows, tq), 0) & (tq - 1)) >> shift
    k_chunk = lax.broadcasted_iota(I32, (rows, tq), 1) >> shift
    kc, kr = keys(i)
    _flash_step(qlat[...], qrot[...], kc, kr, m_scr, l_scr, acc_scr, k_chunk <= q_chunk)
    _finish_mla(tq, l_scr, acc_scr, wuv_ref, omla_ref)
    _mem_attention(mq_ref, mk_ref, mv_ref, omem_ref)


def _attn_scratch(tq):
    rows = MLA_HEADS * tq
    return [pltpu.VMEM((rows, KV_LORA), BF16), pltpu.VMEM((rows, QK_ROPE), BF16),
            pltpu.VMEM((rows, LANES), F32), pltpu.VMEM((rows, LANES), F32), pltpu.VMEM((rows, KV_LORA), F32)]


def _attn_prompt(qn, qr, mq, kcb, krb, mk, mv, wuk, wuv, *, nseq, seq):
    m = qn.shape[0]
    tq = ATT_TQ
    nq = seq // tq
    qrow = lambda w: pl.BlockSpec((tq, w), lambda n, i: (n * nq + i, 0))
    mem = pl.BlockSpec((1,) + mk.shape[1:], lambda n, i: (n, 0, 0))
    out = jax.ShapeDtypeStruct((m, MLA_HEADS * V_HEAD), BF16)
    return pl.pallas_call(
        functools.partial(_attn_prompt_kernel, tq),
        out_shape=(out, out),
        grid=(nseq, nq),
        in_specs=[qrow(Q_NOPE_W), qrow(Q_ROPE_W), qrow(mq.shape[1]),
                  pl.BlockSpec((seq, KV_LORA), lambda n, i: (n, 0)),
                  pl.BlockSpec((seq, QK_ROPE), lambda n, i: (n, 0)),
                  mem, mem, _full(wuk), _full(wuv)],
        out_specs=(qrow(MLA_HEADS * V_HEAD),) * 2,
        scratch_shapes=_attn_scratch(tq),
        compiler_params=_params(2),
        name="attn_prompt",
    )(qn, qr, mq, kcb, krb, mk, mv, wuk, wuv)


def _attn_sample_kernel(tq, past, qn_ref, qr_ref, mq_ref, pc_ref, pr_ref, kc_ref, kr_ref, mk_ref, mv_ref,
                        wuk_ref, wuv_ref, omla_ref, omem_ref, qlat, qrot, m_scr, l_scr, acc_scr):
    _stack_queries(tq, qn_ref, qr_ref, wuk_ref, qlat, qrot, m_scr, l_scr, acc_scr)
    for c in range(past // SAMPLE_KV):
        ks = slice(c * SAMPLE_KV, (c + 1) * SAMPLE_KV)
        _flash_step(qlat[...], qrot[...], pc_ref[0, ks, :].astype(BF16), pr_ref[0, ks, :].astype(BF16),
                    m_scr, l_scr, acc_scr, None)
    _flash_step(qlat[...], qrot[...], kc_ref[...], kr_ref[...], m_scr, l_scr, acc_scr, None)
    _finish_mla(tq, l_scr, acc_scr, wuv_ref, omla_ref)
    _mem_attention(mq_ref, mk_ref, mv_ref, omem_ref)


def _attn_sample(qn, qr, mq, past_c, past_r, kcb, krb, mk, mv, wuk, wuv, *, nseq, seq):
    m = qn.shape[0]
    past = past_c.shape[1]
    qrow = lambda w: pl.BlockSpec((seq, w), lambda n: (n, 0))
    per_seq = lambda a: pl.BlockSpec((1,) + a.shape[1:], lambda n: (n, 0, 0))
    out = jax.ShapeDtypeStruct((m, MLA_HEADS * V_HEAD), BF16)
    return pl.pallas_call(
        functools.partial(_attn_sample_kernel, seq, past),
        out_shape=(out, out),
        grid=(nseq,),
        in_specs=[qrow(Q_NOPE_W), qrow(Q_ROPE_W), qrow(mq.shape[1]), per_seq(past_c), per_seq(past_r),
                  qrow(KV_LORA), qrow(QK_ROPE), per_seq(mk), per_seq(mv), _full(wuk), _full(wuv)],
        out_specs=(qrow(MLA_HEADS * V_HEAD),) * 2,
        scratch_shapes=_attn_scratch(seq),
        compiler_params=_params(),
        name="attn_sample",
    )(qn, qr, mq, past_c, past_r, kcb, krb, mk, mv, wuk, wuv)


ROUTE_ROWS = 8
R_E1, R_E2, R_RANK1, R_RANK2, R_CW1, R_CW2 = range(6)
ROUTER_ROWS = 64
R_GROUP0 = N_EXPERTS


def _split_bf16(x):
    hi = x.astype(BF16)
    lo = (x - hi.astype(F32)).astype(BF16)
    return hi, lo


def _merge_kernel(x_ref, cc_ref, om_ref, ox_ref, wg_ref, bg_ref, wco_ref, wmo_ref, wxo_ref, wout_ref,
                  g1_ref, b1_ref, wr_ref, br_ref, x1_ref, x1b_ref, route_ref, cnt_ref):
    d = x_ref.shape[1]
    tm = x_ref.shape[0]
    x = x_ref[...]
    xb = x.astype(BF16)
    merged = None
    for b, (br_ref_, w_ref_) in enumerate(((cc_ref, wco_ref), (om_ref, wmo_ref), (ox_ref, wxo_ref))):
        gate = _sigmoid(_dot(xb, wg_ref[:, b * d:(b + 1) * d]) + bg_ref[:, b * d:(b + 1) * d])
        term = gate * _dot(br_ref_[...], w_ref_[...])
        merged = term if merged is None else merged + term
    z = DEEPNORM_ALPHA * x + _dot(merged.astype(BF16), wout_ref[...])
    x1 = _layer_norm(z, g1_ref[...], b1_ref[...])
    x1_ref[...] = x1
    x1b_ref[...] = x1.astype(BF16)

    xh, xl = _split_bf16(x1)
    wh, wl = _split_bf16(wr_ref[...])
    lt = _dot_nt(wh, xh) + _dot_nt(wh, xl) + _dot_nt(wl, xh) + br_ref[...]

    g = [lt[R_GROUP0 + k:R_GROUP0 + k + 1, :] for k in range(N_GROUPS)]
    gmax = jnp.maximum(jnp.maximum(g[0], g[1]), jnp.maximum(g[2], g[3]))
    gden = sum(jnp.exp(gk - gmax) for gk in g)
    g_val = 1.0 / gden
    g_idx = jnp.where(g[0] == gmax, 0.0, jnp.where(g[1] == gmax, 1.0, jnp.where(g[2] == gmax, 2.0, 3.0)))
    e_sel = jnp.zeros((EXPERTS_PER_GROUP, tm), F32)
    for k in range(N_GROUPS):
        e_sel = jnp.where(g_idx == float(k), lt[k * EXPERTS_PER_GROUP:(k + 1) * EXPERTS_PER_GROUP, :], e_sel)
    sub = lax.broadcasted_iota(I32, (EXPERTS_PER_GROUP, tm), 0).astype(F32)
    big = float(EXPERTS_PER_GROUP)
    m1 = jnp.max(e_sel, axis=0, keepdims=True)
    i1 = jnp.min(jnp.where(e_sel == m1, sub, big), axis=0, keepdims=True)
    rest = jnp.where(sub == i1, -jnp.inf, e_sel)
    m2 = jnp.max(rest, axis=0, keepdims=True)
    i2 = jnp.min(jnp.where(rest == m2, sub, big), axis=0, keepdims=True)
    eden = jnp.sum(jnp.exp(e_sel - m1), axis=0, keepdims=True)
    p1 = 1.0 / eden
    p2 = jnp.exp(m2 - m1) / eden
    psum = p1 + p2
    cw1 = g_val * (p1 / psum)
    cw2 = g_val * (p2 / psum)
    e1 = g_idx * big + i1
    e2 = g_idx * big + i2

    eio = lax.broadcasted_iota(I32, (N_EXPERTS, tm), 0).astype(F32)
    oh1 = eio == e1
    oh2 = eio == e2
    onehot = jnp.where(oh1, 1.0, jnp.where(oh2, 1.0, 0.0))
    upper = jnp.where(lax.broadcasted_iota(I32, (tm, tm), 0) <= lax.broadcasted_iota(I32, (tm, tm), 1), 1.0, 0.0)
    prefix = _dot(onehot.astype(BF16), upper.astype(BF16))
    rank1 = jnp.sum(jnp.where(oh1, prefix, 0.0), axis=0, keepdims=True) - 1.0
    rank2 = jnp.sum(jnp.where(oh2, prefix, 0.0), axis=0, keepdims=True) - 1.0
    route_ref[R_E1:R_E1 + 1, :] = e1
    route_ref[R_E2:R_E2 + 1, :] = e2
    route_ref[R_RANK1:R_RANK1 + 1, :] = rank1
    route_ref[R_RANK2:R_RANK2 + 1, :] = rank2
    route_ref[R_CW1:R_CW1 + 1, :] = cw1
    route_ref[R_CW2:R_CW2 + 1, :] = cw2
    route_ref[R_CW2 + 1:ROUTE_ROWS, :] = jnp.zeros((ROUTE_ROWS - R_CW2 - 1, tm), F32)
    cnt_ref[0] = jnp.broadcast_to(jnp.sum(onehot, axis=1, keepdims=True), (N_EXPERTS, LANES))


def _merge(x2d, cconv, omla, omem, consts):
    m, d = x2d.shape
    tm = TOKEN_TILE
    nt = m // tm
    row = lambda w: pl.BlockSpec((tm, w), lambda i: (i, 0))
    sds = jax.ShapeDtypeStruct
    return pl.pallas_call(
        _merge_kernel,
        out_shape=(sds((m, d), F32), sds((m, d), BF16), sds((ROUTE_ROWS, m), F32),
                   sds((nt, N_EXPERTS, LANES), F32)),
        grid=(nt,),
        in_specs=[row(d), row(cconv.shape[1]), row(omla.shape[1]), row(omem.shape[1])] + [_full(c) for c in consts],
        out_specs=(row(d), row(d), pl.BlockSpec((ROUTE_ROWS, tm), lambda i: (0, i)),
                   pl.BlockSpec((1, N_EXPERTS, LANES), lambda i: (i, 0, 0))),
        compiler_params=_params(),
        name="merge",
    )(x2d, cconv, omla, omem, *consts)


def _local_positions(route_ref, lsb_ref):
    tm = route_ref.shape[1]
    lsb = lsb_ref[0]
    eio = lax.broadcasted_iota(I32, (N_EXPERTS, tm), 0).astype(F32)
    pos = []
    for r_e, r_rank in ((R_E1, R_RANK1), (R_E2, R_RANK2)):
        e = route_ref[r_e:r_e + 1, :]
        start = jnp.sum(jnp.where(eio == e, lsb, 0.0), axis=0, keepdims=True)
        pos.append((start + route_ref[r_rank:r_rank + 1, :]).astype(I32))
    return pos


def _segment_copies(i, ls_ref, nch_ref, gd_ref, make_copy):
    def seg(e, carry):
        k = i * N_EXPERTS + e
        ls = ls_ref[k]
        gd = gd_ref[k]

        def chunk(c, carry2):
            make_copy(pl.multiple_of(ls + c * ROW_CHUNK, ROW_CHUNK),
                      pl.multiple_of(gd + c * ROW_CHUNK, ROW_CHUNK)).start()
            return carry2

        return lax.fori_loop(0, nch_ref[k], chunk, carry)

    lax.fori_loop(0, N_EXPERTS, seg, 0)


def _zero_fill_copies(ztab_ref, zero_scr, buf_ref, zsem, wait):
    def run(rows, start_of, count_of, n_groups):
        def group(g, carry):
            start = start_of(g)

            def one(c, carry2):
                cp = pltpu.make_async_copy(zero_scr.at[pl.ds(0, rows)],
                                           buf_ref.at[pl.ds(pl.multiple_of(start + c * rows, ROW_CHUNK), rows)], zsem)
                cp.wait() if wait else cp.start()
                return carry2

            return lax.fori_loop(0, count_of(g), one, carry)

        lax.fori_loop(0, n_groups, group, 0)

    n = N_EXPERTS
    run(ZERO_MID, lambda e: ztab_ref[e], lambda e: ztab_ref[n + e], n)
    run(ROW_CHUNK, lambda e: ztab_ref[2 * n + e], lambda e: ztab_ref[3 * n + e], n)
    run(EXPERT_TILE, lambda g: ztab_ref[4 * n], lambda g: ztab_ref[4 * n + 1], 1)


def _dispatch_kernel(nt_p, ls_ref, nch_ref, gd_ref, ntot_ref, ztab_ref, xp_ref, xs_ref, route_ref, lsb_ref,
                     buf_ref, xs_scr, zero_scr, sem, zsem):
    i = pl.program_id(0)
    last = pl.num_programs(0) - 1
    slot = i % 2
    tm = xp_ref.shape[0]

    @pl.when(i == 0)
    def _():
        zero_scr[...] = jnp.zeros(zero_scr.shape, F32)
        _zero_fill_copies(ztab_ref, zero_scr, buf_ref, zsem, wait=False)

    lp1, lp2 = _local_positions(route_ref, lsb_ref)
    rowi = lax.broadcasted_iota(I32, (SORT_ROWS, tm), 0)
    sel1 = rowi == lp1
    sel2 = rowi == lp2
    perm = jnp.where(sel1, 1.0, jnp.where(sel2, 1.0, 0.0)).astype(BF16)

    d = xp_ref.shape[1]

    def sort_rows(x_ref):
        xs_scr[slot, :, 0:d] = _dot(perm, x_ref[...])

    pl.when(i < nt_p)(lambda: sort_rows(xp_ref))
    pl.when(i >= nt_p)(lambda: sort_rows(xs_ref))
    cw = jnp.sum(jnp.where(sel1, route_ref[R_CW1:R_CW1 + 1, :],
                           jnp.where(sel2, route_ref[R_CW2:R_CW2 + 1, :], 0.0)), axis=1, keepdims=True)
    xs_scr[slot, :, d:d + CW_LANES] = jnp.broadcast_to(cw, (SORT_ROWS, CW_LANES))

    def make_copy(s):
        def f(src_row, dst_row):
            return pltpu.make_async_copy(xs_scr.at[s, pl.ds(src_row, ROW_CHUNK)],
                                         buf_ref.at[pl.ds(dst_row, ROW_CHUNK)], sem.at[s])
        return f

    def drain(step, s):
        def body(c, carry):
            make_copy(s)(0, 0).wait()
            return carry
        lax.fori_loop(0, ntot_ref[step], body, 0)

    _segment_copies(i, ls_ref, nch_ref, gd_ref, make_copy(slot))
    pl.when(i > 0)(lambda: drain(i - 1, 1 - slot))

    @pl.when(i == last)
    def _():
        drain(i, slot)
        _zero_fill_copies(ztab_ref, zero_scr, buf_ref, zsem, wait=True)


def _dispatch(seg, ztab, x1b_p, x1b_s, route, lsb, buf_rows):
    d = x1b_p.shape[1]
    tm = TOKEN_TILE
    nt_p, nt_s = x1b_p.shape[0] // tm, x1b_s.shape[0] // tm
    width = d + CW_LANES
    grid_spec = pltpu.PrefetchScalarGridSpec(
        num_scalar_prefetch=5,
        grid=(nt_p + nt_s,),
        in_specs=[pl.BlockSpec((tm, d), lambda i, *_: (jnp.minimum(i, nt_p - 1), 0)),
                  pl.BlockSpec((tm, d), lambda i, *_: (jnp.maximum(i - nt_p, 0), 0)),
                  pl.BlockSpec((ROUTE_ROWS, tm), lambda i, *_: (0, i)),
                  pl.BlockSpec((1, N_EXPERTS, tm), lambda i, *_: (i, 0, 0))],
        out_specs=pl.BlockSpec(memory_space=pl.ANY),
        scratch_shapes=[pltpu.VMEM((2, SORT_ROWS, width), F32), pltpu.VMEM((EXPERT_TILE, width), F32),
                        pltpu.SemaphoreType.DMA((2,)), pltpu.SemaphoreType.DMA(())],
    )
    return pl.pallas_call(
        functools.partial(_dispatch_kernel, nt_p),
        out_shape=jax.ShapeDtypeStruct((buf_rows, width), F32),
        grid_spec=grid_spec,
        compiler_params=_params(),
        name="dispatch",
    )(*seg, ztab, x1b_p, x1b_s, route, lsb)


def _experts_kernel(te_ref, nu_ref, xs_ref, w1_ref, w3_ref, w2_ref, y_ref):
    @pl.when(pl.program_id(0) >= nu_ref[0])
    def _():
        y_ref[...] = jnp.zeros(y_ref.shape, F32)

    @pl.when(pl.program_id(0) < nu_ref[0])
    def _():
        d = y_ref.shape[1]
        x = xs_ref[:, 0:d].astype(BF16)
        cw = xs_ref[:, d:d + 1]
        h1 = _dot(x, w1_ref[0].astype(BF16))
        h3 = _dot(x, w3_ref[0].astype(BF16))
        h = (h1 * _sigmoid(h1) * h3 * cw).astype(BF16)
        y_ref[...] = _dot(h, w2_ref[0].astype(BF16))


def _experts(te, nu, buf, w1, w3, w2):
    rows, width = buf.shape
    tr = EXPERT_TILE
    clamp = lambda i, te_ref, nu_ref: (jnp.minimum(i, nu_ref[0] - 1), 0)
    wspec = lambda w: pl.BlockSpec((1,) + w.shape[1:], lambda i, te_ref, nu_ref: (te_ref[i], 0, 0))
    grid_spec = pltpu.PrefetchScalarGridSpec(
        num_scalar_prefetch=2,
        grid=(rows // tr,),
        in_specs=[pl.BlockSpec((tr, width), clamp), wspec(w1), wspec(w3), wspec(w2)],
        out_specs=pl.BlockSpec((tr, width - CW_LANES), lambda i, te_ref, nu_ref: (i, 0)),
    )
    return pl.pallas_call(
        _experts_kernel,
        out_shape=jax.ShapeDtypeStruct((rows, width - CW_LANES), F32),
        grid_spec=grid_spec,
        compiler_params=_params(),
        name="experts",
    )(te, nu, buf, w1, w3, w2)


def _combine_kernel(ls_ref, nch_ref, gd_ref, ntot_ref, y_hbm, route_ref, lsb_ref, x1_ref, g2_ref, b2_ref,
                    out_ref, ys_scr, sem):
    i = pl.program_id(0)
    last = pl.num_programs(0) - 1
    slot = i % 2
    tm = x1_ref.shape[0]

    def make_copy(s):
        def f(src_row, dst_row):
            return pltpu.make_async_copy(y_hbm.at[pl.ds(dst_row, ROW_CHUNK)],
                                         ys_scr.at[s, pl.ds(src_row, ROW_CHUNK)], sem.at[s])
        return f

    def fetch(step, s):
        ys_scr[s] = jnp.zeros(ys_scr.shape[1:], F32)
        _segment_copies(step, ls_ref, nch_ref, gd_ref, make_copy(s))

    pl.when(i == 0)(lambda: fetch(i, slot))
    pl.when(i < last)(lambda: fetch(i + 1, 1 - slot))
    lp1, lp2 = _local_positions(route_ref, lsb_ref)
    rowi = lax.broadcasted_iota(I32, (SORT_ROWS, tm), 0)
    perm = jnp.where(rowi == lp1, 1.0, jnp.where(rowi == lp2, 1.0, 0.0)).astype(BF16)

    def drain(c, carry):
        make_copy(slot)(0, 0).wait()
        return carry

    lax.fori_loop(0, ntot_ref[i], drain, 0)
    y = _dot_tn(perm, ys_scr[slot].astype(BF16))
    out_ref[...] = _layer_norm(DEEPNORM_ALPHA * x1_ref[...] + y, g2_ref[...], b2_ref[...])


def _combine(seg, ybuf, route, lsb, x1, g2, b2):
    m, d = x1.shape
    tm = TOKEN_TILE
    nt = m // tm
    grid_spec = pltpu.PrefetchScalarGridSpec(
        num_scalar_prefetch=4,
        grid=(nt,),
        in_specs=[pl.BlockSpec(memory_space=pl.ANY),
                  pl.BlockSpec((ROUTE_ROWS, tm), lambda i, *_: (0, i)),
                  pl.BlockSpec((1, N_EXPERTS, tm), lambda i, *_: (i, 0, 0)),
                  pl.BlockSpec((tm, d), lambda i, *_: (i, 0)),
                  pl.BlockSpec(g2.shape, lambda i, *_: (0, 0)),
                  pl.BlockSpec(b2.shape, lambda i, *_: (0, 0))],
        out_specs=pl.BlockSpec((tm, d), lambda i, *_: (i, 0)),
        scratch_shapes=[pltpu.VMEM((2, SORT_ROWS, d), F32), pltpu.SemaphoreType.DMA((2,))],
    )
    return pl.pallas_call(
        _combine_kernel,
        out_shape=jax.ShapeDtypeStruct((m, d), F32),
        grid_spec=grid_spec,
        compiler_params=_params(),
        name="combine",
    )(*seg, ybuf, route, lsb, x1, g2, b2)


def _rope_tables(pos):
    half = QK_ROPE // 2
    inv = jnp.power(ROPE_BASE, -jnp.arange(half, dtype=F32) / half)
    ang = pos.astype(F32)[:, None] * inv[None, :]
    cos, sin = jnp.cos(ang), jnp.sin(ang)
    ck = jnp.concatenate([cos, cos], axis=1)
    sk = jnp.concatenate([-sin, sin], axis=1)
    cq = jnp.tile(ck, (1, MLA_HEADS)) * MLA_SCALE
    sq = jnp.tile(sk, (1, MLA_HEADS)) * MLA_SCALE
    return ck, sk, cq, sq


def _ceil_to(x, k):
    return (x + k - 1) // k * k


def _segment_tables(cnt, n_row_tiles):
    cnt8 = _ceil_to(cnt, ROW_CHUNK)
    lstart = jnp.cumsum(cnt8, axis=1) - cnt8
    filled = jnp.sum(cnt8, axis=0)
    tot = _ceil_to(filled, EXPERT_TILE)
    ends = jnp.cumsum(tot)
    gdst = (ends - tot)[None, :] + jnp.cumsum(cnt8, axis=0) - cnt8
    nch = cnt8 // ROW_CHUNK
    n_used = ends[-1] // EXPERT_TILE
    tile_row = jnp.minimum(jnp.arange(n_row_tiles, dtype=I32), n_used - 1) * EXPERT_TILE
    tile_expert = jnp.minimum(jnp.sum((ends[None, :] <= tile_row[:, None]).astype(I32), axis=1), N_EXPERTS - 1)
    tail0 = ends - tot + filled
    tail = tot - filled
    n_mid = tail // ZERO_MID
    ztab = jnp.concatenate([tail0, n_mid, tail0 + n_mid * ZERO_MID, (tail % ZERO_MID) // ROW_CHUNK,
                            ends[-1:], (n_row_tiles - n_used).reshape(1)]).astype(I32)
    return lstart, gdst, nch, jnp.sum(nch, axis=1), tile_expert, n_used.reshape(1).astype(I32), ztab


def kernel(x_prompt, x_sample, cache_conv, cache_mla_latent, cache_mla_krope, cache_mem_k, cache_mem_v,
           mem_prompt, w_in, b_gate, w_dw, b_dw, g_cn, b_cn, w_co, g_qa, w_uq, g_kva, w_uk, w_uv, w_mo,
           w_mem_k, w_mem_v, w_mem_o, w_out, g_ln1, b_ln1, w_group, b_group, w_router, b_router,
           w_e1, w_e3, w_e2, g_ln2, b_ln2):
    n_p, s_p, d = x_prompt.shape
    n_s, s_s, _ = x_sample.shape
    past = cache_mla_latent.shape[2]
    n_mem = mem_prompt.shape[1]
    assert w_in.shape[0] == DEPTH == 1
    assert s_p % TOKEN_TILE == 0 and s_p % ATT_TQ == 0 and TOKEN_TILE % s_s == 0 and (n_s * s_s) % TOKEN_TILE == 0
    assert s_s <= CHUNK and past % CHUNK == 0 and past % SAMPLE_KV == 0
    mem_w = MEM_HEADS * MEM_HEAD_DIM
    row2 = lambda v: v.reshape(1, -1)

    wl = w_in[0]
    sp = (C_GLU, C_GLU + Q_LORA, C_GLU + Q_LORA + KV_LORA + QK_ROPE, C_GLU + Q_LORA + KV_LORA + QK_ROPE + mem_w)
    w_glu, w_qd, w_kvd, w_mq, w_gate = wl[:, :sp[0]], wl[:, sp[0]:sp[1]], wl[:, sp[1]:sp[2]], wl[:, sp[2]:sp[3]], wl[:, sp[3]:]
    half = QK_ROPE // 2
    w_kr = w_kvd[:, KV_LORA:]
    w_krs = jnp.concatenate([w_kr[:, half:], w_kr[:, :half]], axis=1)
    pad = jnp.zeros((d, C_MQ - (C_KRS + QK_ROPE)), F32)
    w_a = jnp.concatenate([w_glu, w_qd, w_kvd, w_krs, pad, w_mq], axis=1).astype(BF16)
    wq = w_uq[0].reshape(Q_LORA, MLA_HEADS, QK_NOPE + QK_ROPE)
    wq_r = wq[:, :, QK_NOPE:]
    wq_rs = jnp.concatenate([wq_r[:, :, half:], wq_r[:, :, :half]], axis=2)
    w_uq2 = jnp.concatenate([wq[:, :, :QK_NOPE].reshape(Q_LORA, -1), wq_r.reshape(Q_LORA, -1),
                             wq_rs.reshape(Q_LORA, -1)], axis=1).astype(BF16)
    wdw8 = jnp.broadcast_to(w_dw[0][:, None, :], (CONV_WIDTH, SUBLANES, CONV_CH))
    in_consts = (w_a, w_uq2, wdw8, row2(b_dw[0]), row2(g_cn[0]), row2(b_cn[0]), row2(g_qa[0]), row2(g_kva[0]))
    wuk_t = jnp.transpose(w_uk[0], (1, 2, 0)).astype(BF16)
    wuv_h = jnp.transpose(w_uv[0], (1, 0, 2)).astype(BF16)
    w_r = jnp.zeros((ROUTER_ROWS, d), F32).at[:N_EXPERTS].set(w_router[0].T).at[R_GROUP0:R_GROUP0 + N_GROUPS].set(w_group[0].T)
    b_r = jnp.zeros((ROUTER_ROWS, 1), F32).at[:N_EXPERTS, 0].set(b_router[0]).at[R_GROUP0:R_GROUP0 + N_GROUPS, 0].set(b_group[0])
    merge_consts = (w_gate.astype(BF16), row2(b_gate[0]), w_co[0].astype(BF16), w_mo[0].astype(BF16),
                    w_mem_o[0].astype(BF16), w_out[0].astype(BF16), row2(g_ln1[0]), row2(b_ln1[0]), w_r, b_r)

    xp = x_prompt.reshape(n_p * s_p, d)
    mk_p, mv_p = _mem_proj(mem_prompt, w_mem_k[0].astype(BF16), w_mem_v[0].astype(BF16))
    tabs_p = _rope_tables(jnp.arange(s_p, dtype=I32))
    (cc_p, nconv_p, ckv_p, kcb_p, krot_p, krb_p, qn_p, qr_p, mq_p) = _inproj(
        xp, None, in_consts, tabs_p, nseg=1, seg_len=TOKEN_TILE, tiles_per_seq=s_p // TOKEN_TILE, nseq=n_p)
    omla_p, omem_p = _attn_prompt(qn_p, qr_p, mq_p, kcb_p, krb_p, mk_p, mv_p, wuk_t, wuv_h, nseq=n_p, seq=s_p)
    x1_p, x1b_p, route_p, cnt_p = _merge(xp, cc_p, omla_p, omem_p, merge_consts)

    xs = x_sample.reshape(n_s * s_s, d)
    seqs_per_tile = TOKEN_TILE // s_s
    tabs_s = tuple(jnp.tile(t, (seqs_per_tile, 1)) for t in _rope_tables(past + jnp.arange(s_s, dtype=I32)))
    (cc_s, nconv_s, ckv_s, kcb_s, krot_s, krb_s, qn_s, qr_s, mq_s) = _inproj(
        xs, cache_conv[0], in_consts, tabs_s, nseg=seqs_per_tile, seg_len=s_s, tiles_per_seq=1, nseq=n_s)
    omla_s, omem_s = _attn_sample(qn_s, qr_s, mq_s, cache_mla_latent[0], cache_mla_krope[0], kcb_s, krb_s,
                                  cache_mem_k[0].reshape(n_s, n_mem * MEM_HEADS, MEM_HEAD_DIM),
                                  cache_mem_v[0].reshape(n_s, n_mem * MEM_HEADS, MEM_HEAD_DIM),
                                  wuk_t, wuv_h, nseq=n_s, seq=s_s)
    x1_s, x1b_s, route_s, cnt_s = _merge(xs, cc_s, omla_s, omem_s, merge_consts)

    nt_p, nt_s = cnt_p.shape[0], cnt_s.shape[0]
    cnt = jnp.concatenate([cnt_p[:, :, 0], cnt_s[:, :, 0]], axis=0).astype(I32)
    m_tot = (nt_p + nt_s) * TOKEN_TILE
    buf_rows = _ceil_to(2 * m_tot + (nt_p + nt_s) * N_EXPERTS * (ROW_CHUNK - 1) + N_EXPERTS * (EXPERT_TILE - 1), EXPERT_TILE)
    lstart, gdst, nch, ntot, tile_expert, n_used, ztab = _segment_tables(cnt, buf_rows // EXPERT_TILE)
    lsb = jnp.broadcast_to(lstart.astype(F32)[:, :, None], lstart.shape + (TOKEN_TILE,))
    seg = lambda a, b: (lstart[a:b].reshape(-1), nch[a:b].reshape(-1), gdst[a:b].reshape(-1), ntot[a:b])
    seg_p, seg_s = seg(0, nt_p), seg(nt_p, nt_p + nt_s)
    buf = _dispatch(seg(0, nt_p + nt_s), ztab, x1b_p, x1b_s, jnp.concatenate([route_p, route_s], axis=1), lsb, buf_rows)
    n_e = N_EXPERTS
    ybuf = _experts(tile_expert, n_used, buf, w_e1[0].reshape((n_e,) + w_e1.shape[3:]),
                    w_e3[0].reshape((n_e,) + w_e3.shape[3:]), w_e2[0].reshape((n_e,) + w_e2.shape[3:]))
    g2, b2 = row2(g_ln2[0]), row2(b_ln2[0])
    y_p = _combine(seg_p, ybuf, route_p, lsb[:nt_p], x1_p, g2, b2)
    y_s = _combine(seg_s, ybuf, route_s, lsb[nt_p:], x1_s, g2, b2)

    lead = lambda a, n, t: a.reshape((1, n, t) + a.shape[1:])
    return (y_p.reshape(n_p, s_p, d), y_s.reshape(n_s, s_s, d),
            nconv_p[None], lead(ckv_p, n_p, s_p), lead(krot_p, n_p, s_p),
            mk_p.reshape(1, n_p, n_mem, MEM_HEADS, MEM_HEAD_DIM), mv_p.reshape(1, n_p, n_mem, MEM_HEADS, MEM_HEAD_DIM),
            nconv_s[None], lead(ckv_s, n_s, s_s), lead(krot_s, n_s, s_s))
```

```python
import functools

import jax
import jax.numpy as jnp
from jax import lax
from jax.experimental import pallas as pl
from jax.experimental.pallas import tpu as pltpu

F32 = jnp.float32
BF16 = jnp.bfloat16
U32 = jnp.uint32
I32 = jnp.int32

DEPTH = 1
CHUNK = 64
EPS = 1e-5
CONV_CH = 512
CONV_WIDTH = 31
HALO = CONV_WIDTH - 1
MLA_HEADS = 8
Q_LORA = 384
KV_LORA = 256
QK_NOPE = 64
QK_ROPE = 32
V_HEAD = 64
ROPE_BASE = 10000.0
MLA_SCALE = (QK_NOPE + QK_ROPE) ** -0.5
MEM_HEADS = 4
MEM_HEAD_DIM = 128
MEM_SCALE = MEM_HEAD_DIM ** -0.5
N_GROUPS = 4
EXPERTS_PER_GROUP = 8
N_EXPERTS = N_GROUPS * EXPERTS_PER_GROUP
DEEPNORM_ALPHA = (2.0 * DEPTH) ** 0.25

SUBLANES = 8
LANES = 128
TOKEN_TILE = 512
HALO_ROWS = 32
CONV_ROWS = 32
ATT_TQ = 256
SAMPLE_KV = 512
SAMPLE_SEQS = 2
ROW_CHUNK = SUBLANES
EXPERT_TILE = 512
ZERO_MID = 64
SORT_ROWS = 2 * TOKEN_TILE + N_EXPERTS * ROW_CHUNK
CW_LANES = LANES
VMEM_LIMIT = 56 * 1024 * 1024


def _dot(a, b):
    return jnp.dot(a, b, preferred_element_type=F32)


def _dot_nt(a, b):
    return lax.dot_general(a, b, (((1,), (1,)), ((), ())), preferred_element_type=F32)


def _dot_tn(a, b):
    return lax.dot_general(a, b, (((0,), (0,)), ((), ())), preferred_element_type=F32)


def _sigmoid(x):
    return 1.0 / (1.0 + jnp.exp(-x))


def _layer_norm(x, g, b):
    mu = jnp.mean(x, axis=-1, keepdims=True)
    d = x - mu
    var = jnp.mean(d * d, axis=-1, keepdims=True)
    return d * lax.rsqrt(var + EPS) * g + b


def _rms_norm(x, g):
    return x * lax.rsqrt(jnp.mean(x * x, axis=-1, keepdims=True) + EPS) * g


def _full(a):
    nd = a.ndim
    return pl.BlockSpec(a.shape, lambda *_: (0,) * nd)


def _params(n_axes=1, vmem=VMEM_LIMIT):
    return pltpu.CompilerParams(dimension_semantics=("arbitrary",) * n_axes, vmem_limit_bytes=vmem)


def _head_rows(h, n_tok):
    return pl.ds(h, n_tok, stride=MEM_HEADS)


def _mem_proj_kernel(m_ref, wk_ref, wv_ref, k_ref, v_ref):
    m = m_ref[0].astype(BF16)
    t = m.shape[0]
    k = _dot(m, wk_ref[...])
    v = _dot(m, wv_ref[...])
    for h in range(MEM_HEADS):
        cols = slice(h * MEM_HEAD_DIM, (h + 1) * MEM_HEAD_DIM)
        k_ref.at[0][_head_rows(h, t), :] = k[:, cols]
        v_ref.at[0][_head_rows(h, t), :] = v[:, cols]


def _mem_proj(mem, wk, wv):
    n, t, d = mem.shape
    out = jax.ShapeDtypeStruct((n, t * MEM_HEADS, MEM_HEAD_DIM), F32)
    return pl.pallas_call(
        _mem_proj_kernel,
        out_shape=(out, out),
        grid=(n,),
        in_specs=[pl.BlockSpec((1, t, d), lambda i: (i, 0, 0)), _full(wk), _full(wv)],
        out_specs=(pl.BlockSpec((1, t * MEM_HEADS, MEM_HEAD_DIM), lambda i: (i, 0, 0)),) * 2,
        compiler_params=_params(),
        name="mem_proj",
    )(mem, wk, wv)


C_GLU = 2 * CONV_CH
C_QD = C_GLU
C_KV = C_QD + Q_LORA
C_KR = C_KV + KV_LORA
C_KRS = C_KR + QK_ROPE
C_MQ = 1792
C_END = C_MQ + MEM_HEADS * MEM_HEAD_DIM
Q_NOPE_W = MLA_HEADS * QK_NOPE
Q_ROPE_W = MLA_HEADS * QK_ROPE
HEAD_PAD = LANES
P_KRB = C_KR
P_KRSB = P_KRB + HEAD_PAD
P_MQ = P_KRSB + HEAD_PAD
P_END = P_MQ + MEM_HEADS * MEM_HEAD_DIM
QK_PAD_W = MLA_HEADS * HEAD_PAD
V_ALL = MLA_HEADS * V_HEAD


def _inproj_kernel(nseg, seg_len, tiles_per_seq, has_state, *refs):
    it = iter(refs)
    x_ref = next(it)
    st_ref = next(it) if has_state else None
    if has_state:
        (w_ref, wuq_ref, wdw_ref, bdw_ref, gcn_ref, bcn_ref, gqa_ref, gkva_ref,
         ck_ref, sk_ref, cq_ref, sq_ref,
         cconv_ref, nconv_ref, ckv_ref, kcb_ref, krot_ref, krb_ref, qn_ref, qr_ref, mq_ref,
         win) = it
    else:
        (w_ref, wqa_ref, wqb_ref, wukp_ref, wuvt_ref, wdw_ref, bdw_ref, gcn_ref, bcn_ref, gqa_ref, gkva_ref,
         ca_ref, sb_ref, ckp_ref, skp_ref,
         cconv_ref, nconv_ref, ckv_ref, krot_ref, qp_ref, kp_ref, vt_ref, mq_ref,
         win) = it

    xb = x_ref[...].astype(BF16)
    ab = _dot(xb, w_ref[:, 0:C_GLU])
    u = ab[:, :CONV_CH] * _sigmoid(ab[:, CONV_CH:])

    if has_state:
        for s in range(nseg):
            win[s, HALO_ROWS - HALO:HALO_ROWS, :] = st_ref[s]
            win[s, HALO_ROWS:HALO_ROWS + seg_len, :] = u[s * seg_len:(s + 1) * seg_len]
    else:
        j = pl.program_id(0) % tiles_per_seq

        @pl.when(j == 0)
        def _():
            win[0, 0:HALO_ROWS, :] = jnp.zeros((HALO_ROWS, CONV_CH), F32)

        @pl.when(j != 0)
        def _():
            win[0, 0:HALO_ROWS, :] = win[0, seg_len:seg_len + HALO_ROWS, :]

        win[0, HALO_ROWS:HALO_ROWS + seg_len, :] = u

    g_cn = gcn_ref[...]
    b_cn = bcn_ref[...]
    b_dw = bdw_ref[...]
    base = HALO_ROWS - HALO
    span = HALO_ROWS + CONV_ROWS
    for s in range(nseg):
        for rc in range(seg_len // CONV_ROWS):
            r0 = rc * CONV_ROWS
            xw = win[s, r0:r0 + span, :]
            acc = jnp.zeros((CONV_ROWS // SUBLANES, SUBLANES, CONV_CH), F32) + b_dw
            for b in range(SUBLANES):
                xsh = xw if b == 0 else xw[b:b + span - SUBLANES, :]
                for a in range(span // SUBLANES):
                    k = a * SUBLANES + b - base
                    if 0 <= k < CONV_WIDTH:
                        xk = xsh[a * SUBLANES:a * SUBLANES + CONV_ROWS, :]
                        acc = acc + xk.reshape(CONV_ROWS // SUBLANES, SUBLANES, CONV_CH) * wdw_ref[k]
            y = _layer_norm(acc.reshape(CONV_ROWS, CONV_CH), g_cn, b_cn)
            y = y * _sigmoid(y)
            cconv_ref[s * seg_len + r0:s * seg_len + r0 + CONV_ROWS, :] = y.astype(BF16)
        nconv_ref[s] = win[s, seg_len + HALO_ROWS - HALO:seg_len + HALO_ROWS, :]

    rest = _dot(xb, w_ref[:, C_QD:w_ref.shape[1]])
    o = -C_QD
    qd = rest[:, C_QD + o:C_KV + o]
    qn = _rms_norm(qd, gqa_ref[...]).astype(BF16)
    if not has_state:
        qp = (_dot(qn, wqa_ref[...]) * _rep(ca_ref[...], MLA_HEADS)
              + _dot(qn, wqb_ref[...]) * _rep(sb_ref[...], MLA_HEADS))
        qp_ref[...] = qp.astype(BF16)
        ckv = _rms_norm(rest[:, C_KV + o:C_KR + o], gkva_ref[...])
        ckv_ref[...] = ckv
        cb = ckv.astype(BF16)
        kr = (rest[:, P_KRB + o:P_KRB + HEAD_PAD + o] * ckp_ref[...]
              + rest[:, P_KRSB + o:P_KRSB + HEAD_PAD + o] * skp_ref[...])
        krot_ref[...] = kr[:, QK_NOPE:QK_NOPE + QK_ROPE]
        kp_ref[...] = (_dot(cb, wukp_ref[...]) + _rep(kr, MLA_HEADS)).astype(BF16)
        vt = _dot_nt(wuvt_ref[...], cb).astype(BF16)
        for c in range(vt_ref.shape[1]):
            vt_ref[0, c] = vt[:, c * ATT_TQ:(c + 1) * ATT_TQ]
        mq_ref[...] = rest[:, P_MQ + o:P_END + o].astype(BF16)
        return
    q = _dot(qn, wuq_ref[...])
    qn_ref[...] = (q[:, :Q_NOPE_W] * MLA_SCALE).astype(BF16)
    qr_ref[...] = (q[:, Q_NOPE_W:Q_NOPE_W + Q_ROPE_W] * cq_ref[...]
                   + q[:, Q_NOPE_W + Q_ROPE_W:] * sq_ref[...]).astype(BF16)
    ckv = _rms_norm(rest[:, C_KV + o:C_KR + o], gkva_ref[...])
    ckv_ref[...] = ckv
    kcb_ref[...] = ckv.astype(BF16)
    kr = rest[:, C_KR + o:C_KRS + o] * ck_ref[...] + rest[:, C_KRS + o:C_KRS + QK_ROPE + o] * sk_ref[...]
    krot_ref[...] = kr
    krb_ref[...] = kr.astype(BF16)
    mq_ref[...] = rest[:, C_MQ + o:C_END + o].astype(BF16)


def _inproj(x2d, state, consts, tables, *, nseg, seg_len, tiles_per_seq, nseq):
    m, d = x2d.shape
    tm = nseg * seg_len
    nt = m // tm
    has_state = state is not None
    tbl_tiles = tables[0].shape[0] // tm
    in_specs = [pl.BlockSpec((tm, d), lambda i: (i, 0))]
    args = [x2d]
    if has_state:
        in_specs.append(pl.BlockSpec((nseg, HALO, CONV_CH), lambda i: (i, 0, 0)))
        args.append(state)
    in_specs += [_full(c) for c in consts]
    args += list(consts)
    for t in tables:
        in_specs.append(pl.BlockSpec((tm, t.shape[1]), lambda i: (i % tbl_tiles, 0)))
        args.append(t)
    row = lambda w: pl.BlockSpec((tm, w), lambda i: (i, 0))
    if has_state:
        nconv_spec = pl.BlockSpec((nseg, HALO, CONV_CH), lambda i: (i, 0, 0))
    else:
        nconv_spec = pl.BlockSpec((1, HALO, CONV_CH), lambda i: (i // tiles_per_seq, 0, 0))
    sds = jax.ShapeDtypeStruct
    mem_w = MEM_HEADS * MEM_HEAD_DIM
    if has_state:
        out_shape = (sds((m, CONV_CH), BF16), sds((nseq, HALO, CONV_CH), F32), sds((m, KV_LORA), F32),
                     sds((m, KV_LORA), BF16), sds((m, QK_ROPE), F32), sds((m, QK_ROPE), BF16),
                     sds((m, Q_NOPE_W), BF16), sds((m, Q_ROPE_W), BF16), sds((m, mem_w), BF16))
        out_specs = (row(CONV_CH), nconv_spec, row(KV_LORA), row(KV_LORA), row(QK_ROPE), row(QK_ROPE),
                     row(Q_NOPE_W), row(Q_ROPE_W), row(mem_w))
    else:
        kv_tiles = tm // ATT_TQ
        out_shape = (sds((m, CONV_CH), BF16), sds((nseq, HALO, CONV_CH), F32), sds((m, KV_LORA), F32),
                     sds((m, QK_ROPE), F32), sds((m, QK_PAD_W), BF16), sds((m, QK_PAD_W), BF16),
                     sds((nseq, tiles_per_seq * kv_tiles, V_ALL, ATT_TQ), BF16), sds((m, mem_w), BF16))
        out_specs = (row(CONV_CH), nconv_spec, row(KV_LORA), row(QK_ROPE), row(QK_PAD_W), row(QK_PAD_W),
                     pl.BlockSpec((1, kv_tiles, V_ALL, ATT_TQ),
                                  lambda i: (i // tiles_per_seq, i % tiles_per_seq, 0, 0)),
                     row(mem_w))
    return pl.pallas_call(
        functools.partial(_inproj_kernel, nseg, seg_len, tiles_per_seq, has_state),
        out_shape=out_shape,
        grid=(nt,),
        in_specs=in_specs,
        out_specs=out_specs,
        scratch_shapes=[pltpu.VMEM((nseg, HALO_ROWS + seg_len, CONV_CH), F32)],
        compiler_params=_params(),
        name="inproj_prompt" if not has_state else "inproj_sample",
    )(*args)


def _rep(x, n):
    return x if n == 1 else jnp.concatenate([x] * n, axis=1)


def _mem_attention(mq_ref, mk_ref, mv_ref, o_ref):
    nsq = mk_ref.shape[0]
    tq = mq_ref.shape[0] // nsq
    n_tok = mk_ref.shape[1] // MEM_HEADS
    cols = [slice(h * MEM_HEAD_DIM, (h + 1) * MEM_HEAD_DIM) for h in range(MEM_HEADS)]
    pairs = [(n, h) for n in range(nsq) for h in range(MEM_HEADS)]
    rows = lambda n: slice(n * tq, (n + 1) * tq)
    scores = [_dot_nt(mq_ref[rows(n), cols[h]], mk_ref.at[n][_head_rows(h, n_tok), :].astype(BF16))
              for n, h in pairs]
    probs, invs = [], []
    for s in scores:
        s = s * MEM_SCALE
        p = jnp.exp(s - jnp.max(s, axis=1, keepdims=True))
        invs.append(1.0 / jnp.sum(p, axis=1, keepdims=True))
        probs.append(p.astype(BF16))
    for (n, h), p, inv in zip(pairs, probs, invs):
        mv = mv_ref.at[n][_head_rows(h, n_tok), :].astype(BF16)
        o_ref[rows(n), cols[h]] = (_dot(p, mv) * inv).astype(BF16)


def _attn_prompt_kernel(tq, q_ref, mq_ref, k_ref, vt_ref, mk_ref, mv_ref, omla_ref, omem_ref,
                        m_scr, l_scr, acc_scr):
    i = pl.program_id(1)
    m_scr[...] = jnp.full(m_scr.shape, -jnp.inf, F32)
    l_scr[...] = jnp.zeros(l_scr.shape, F32)
    acc_scr[...] = jnp.zeros(acc_scr.shape, F32)

    def step(j, mask):
        k0 = pl.multiple_of(j * tq, tq)
        scores = []
        for h in range(MLA_HEADS):
            cols = slice(h * HEAD_PAD, (h + 1) * HEAD_PAD)
            scores.append(_dot_nt(k_ref[pl.ds(k0, tq), cols], q_ref[:, cols]))
        probs, alphas = [], []
        for h in range(MLA_HEADS):
            s = scores[h] if mask is None else jnp.where(mask, scores[h], -1e30)
            m_prev = m_scr[h]
            m_new = jnp.maximum(m_prev, jnp.max(s, axis=0, keepdims=True))
            alpha = jnp.exp2(m_prev - m_new)
            p = jnp.exp2(s - m_new)
            l_scr[h] = alpha * l_scr[h] + jnp.sum(p, axis=0, keepdims=True)
            m_scr[h] = m_new
            probs.append(p.astype(BF16))
            alphas.append(alpha)
        for h in range(MLA_HEADS):
            rows = slice(h * V_HEAD, (h + 1) * V_HEAD)
            acc_scr[rows, :] = alphas[h] * acc_scr[rows, :] + _dot(vt_ref[0, j, rows, :], probs[h])

    def body(j, carry):
        step(j, None)
        return carry

    lax.fori_loop(0, i, body, 0)
    shift = CHUNK.bit_length() - 1
    k_chunk = lax.broadcasted_iota(I32, (tq, tq), 0) >> shift
    q_chunk = lax.broadcasted_iota(I32, (tq, tq), 1) >> shift
    step(i, k_chunk <= q_chunk)
    o_t = jnp.concatenate([acc_scr[h * V_HEAD:(h + 1) * V_HEAD, :] * (1.0 / l_scr[h]) for h in range(MLA_HEADS)],
                          axis=0)
    omla_ref[...] = o_t.T.astype(BF16)
    _mem_attention(mq_ref, mk_ref, mv_ref, omem_ref)


def _attn_prompt(qp, mq, kp, vt, mk, mv, *, nseq, seq):
    m = qp.shape[0]
    tq = ATT_TQ
    nq = seq // tq
    qrow = lambda w: pl.BlockSpec((tq, w), lambda n, i: (n * nq + i, 0))
    mem = pl.BlockSpec((1,) + mk.shape[1:], lambda n, i: (n, 0, 0))
    out = jax.ShapeDtypeStruct((m, V_ALL), BF16)
    return pl.pallas_call(
        functools.partial(_attn_prompt_kernel, tq),
        out_shape=(out, out),
        grid=(nseq, nq),
        in_specs=[qrow(QK_PAD_W), qrow(mq.shape[1]),
                  pl.BlockSpec((seq, QK_PAD_W), lambda n, i: (n, 0)),
                  pl.BlockSpec((1,) + vt.shape[1:], lambda n, i: (n, 0, 0, 0)),
                  mem, mem],
        out_specs=(qrow(V_ALL),) * 2,
        scratch_shapes=[pltpu.VMEM((MLA_HEADS, 1, tq), F32), pltpu.VMEM((MLA_HEADS, 1, tq), F32),
                        pltpu.VMEM((V_ALL, tq), F32)],
        compiler_params=_params(2),
        name="attn_prompt",
    )(qp, mq, kp, vt, mk, mv)


def _attn_sample_kernel(tq, past, qn_ref, qr_ref, mq_ref, pc_ref, pr_ref, kc_ref, kr_ref, mk_ref, mv_ref,
                        wuk_ref, wuv_ref, omla_ref, omem_ref, qlat, qrot):
    nsq = pc_ref.shape[0]
    seqs = range(nsq)
    for n in seqs:
        tok = slice(n * tq, (n + 1) * tq)
        for h in range(MLA_HEADS):
            qlat[n, h * tq:(h + 1) * tq, :] = _dot(qn_ref[tok, h * QK_NOPE:(h + 1) * QK_NOPE], wuk_ref[h]).astype(BF16)
            qrot[n, h * tq:(h + 1) * tq, :] = qr_ref[tok, h * QK_ROPE:(h + 1) * QK_ROPE]
    keys, scores = [], []
    for n in seqs:
        tok = slice(n * tq, (n + 1) * tq)
        kn = []
        for c in range(past // SAMPLE_KV):
            ks = slice(c * SAMPLE_KV, (c + 1) * SAMPLE_KV)
            kn.append((pc_ref[n, ks, :].astype(BF16), pr_ref[n, ks, :].astype(BF16)))
        kn.append((kc_ref[tok, :], kr_ref[tok, :]))
        keys.append(kn)
        scores.append([_dot_nt(qlat[n], kc) + _dot_nt(qrot[n], kr) for kc, kr in kn])
    probs, invs = [], []
    for n in seqs:
        m = functools.reduce(jnp.maximum, [jnp.max(s, axis=1, keepdims=True) for s in scores[n]])
        pn = [jnp.exp(s - m) for s in scores[n]]
        invs.append(1.0 / sum(jnp.sum(p, axis=1, keepdims=True) for p in pn))
        probs.append([p.astype(BF16) for p in pn])
    accs = [sum(_dot(p, kc) for p, (kc, _) in zip(probs[n], keys[n])) for n in seqs]
    for n in seqs:
        tok = slice(n * tq, (n + 1) * tq)
        for h in range(MLA_HEADS):
            rows = slice(h * tq, (h + 1) * tq)
            o_lat = (accs[n][rows, :] * invs[n][rows, :]).astype(BF16)
            omla_ref[tok, h * V_HEAD:(h + 1) * V_HEAD] = _dot(o_lat, wuv_ref[h]).astype(BF16)
    _mem_attention(mq_ref, mk_ref, mv_ref, omem_ref)


def _attn_sample(qn, qr, mq, past_c, past_r, kcb, krb, mk, mv, wuk, wuv, *, nseq, seq):
    m = qn.shape[0]
    past = past_c.shape[1]
    nsq = SAMPLE_SEQS
    qrow = lambda w: pl.BlockSpec((nsq * seq, w), lambda n: (n, 0))
    per_seq = lambda a: pl.BlockSpec((nsq,) + a.shape[1:], lambda n: (n, 0, 0))
    out = jax.ShapeDtypeStruct((m, MLA_HEADS * V_HEAD), BF16)
    return pl.pallas_call(
        functools.partial(_attn_sample_kernel, seq, past),
        out_shape=(out, out),
        grid=(nseq // nsq,),
        in_specs=[qrow(Q_NOPE_W), qrow(Q_ROPE_W), qrow(mq.shape[1]), per_seq(past_c), per_seq(past_r),
                  qrow(KV_LORA), qrow(QK_ROPE), per_seq(mk), per_seq(mv), _full(wuk), _full(wuv)],
        out_specs=(qrow(MLA_HEADS * V_HEAD),) * 2,
        scratch_shapes=[pltpu.VMEM((nsq, MLA_HEADS * seq, KV_LORA), BF16),
                        pltpu.VMEM((nsq, MLA_HEADS * seq, QK_ROPE), BF16)],
        compiler_params=_params(),
        name="attn_sample",
    )(qn, qr, mq, past_c, past_r, kcb, krb, mk, mv, wuk, wuv)


ROUTE_ROWS = 8
R_E1, R_E2, R_RANK1, R_RANK2, R_CW1, R_CW2 = range(6)
ROUTER_ROWS = 64
R_GROUP0 = N_EXPERTS


def _split_bf16(x):
    hi = x.astype(BF16)
    lo = (x - hi.astype(F32)).astype(BF16)
    return hi, lo


def _merge_kernel(x_ref, cc_ref, om_ref, ox_ref, wg_ref, bg_ref, wco_ref, wmo_ref, wxo_ref, wout_ref,
                  g1_ref, b1_ref, wr_ref, br_ref, x1_ref, x1b_ref, route_ref, cnt_ref):
    d = x_ref.shape[1]
    tm = x_ref.shape[0]
    x = x_ref[...]
    xb = x.astype(BF16)
    merged = None
    for b, (br_ref_, w_ref_) in enumerate(((cc_ref, wco_ref), (om_ref, wmo_ref), (ox_ref, wxo_ref))):
        gate = _sigmoid(_dot(xb, wg_ref[:, b * d:(b + 1) * d]) + bg_ref[:, b * d:(b + 1) * d])
        term = gate * _dot(br_ref_[...], w_ref_[...])
        merged = term if merged is None else merged + term
    z = DEEPNORM_ALPHA * x + _dot(merged.astype(BF16), wout_ref[...])
    x1 = _layer_norm(z, g1_ref[...], b1_ref[...])
    x1_ref[...] = x1
    x1b_ref[...] = x1.astype(BF16)

    xh, xl = _split_bf16(x1)
    wh, wl = _split_bf16(wr_ref[...])
    lt = _dot_nt(wh, xh) + _dot_nt(wh, xl) + _dot_nt(wl, xh) + br_ref[...]

    g = [lt[R_GROUP0 + k:R_GROUP0 + k + 1, :] for k in range(N_GROUPS)]
    gmax = jnp.maximum(jnp.maximum(g[0], g[1]), jnp.maximum(g[2], g[3]))
    gden = sum(jnp.exp(gk - gmax) for gk in g)
    g_val = 1.0 / gden
    g_idx = jnp.where(g[0] == gmax, 0.0, jnp.where(g[1] == gmax, 1.0, jnp.where(g[2] == gmax, 2.0, 3.0)))
    e_sel = jnp.zeros((EXPERTS_PER_GROUP, tm), F32)
    for k in range(N_GROUPS):
        e_sel = jnp.where(g_idx == float(k), lt[k * EXPERTS_PER_GROUP:(k + 1) * EXPERTS_PER_GROUP, :], e_sel)
    sub = lax.broadcasted_iota(I32, (EXPERTS_PER_GROUP, tm), 0).astype(F32)
    big = float(EXPERTS_PER_GROUP)
    m1 = jnp.max(e_sel, axis=0, keepdims=True)
    i1 = jnp.min(jnp.where(e_sel == m1, sub, big), axis=0, keepdims=True)
    rest = jnp.where(sub == i1, -jnp.inf, e_sel)
    m2 = jnp.max(rest, axis=0, keepdims=True)
    i2 = jnp.min(jnp.where(rest == m2, sub, big), axis=0, keepdims=True)
    eden = jnp.sum(jnp.exp(e_sel - m1), axis=0, keepdims=True)
    p1 = 1.0 / eden
    p2 = jnp.exp(m2 - m1) / eden
    psum = p1 + p2
    cw1 = g_val * (p1 / psum)
    cw2 = g_val * (p2 / psum)
    e1 = g_idx * big + i1
    e2 = g_idx * big + i2

    eio = lax.broadcasted_iota(I32, (N_EXPERTS, tm), 0).astype(F32)
    oh1 = eio == e1
    oh2 = eio == e2
    onehot = jnp.where(oh1, 1.0, jnp.where(oh2, 1.0, 0.0))
    upper = jnp.where(lax.broadcasted_iota(I32, (tm, tm), 0) <= lax.broadcasted_iota(I32, (tm, tm), 1), 1.0, 0.0)
    prefix = _dot(onehot.astype(BF16), upper.astype(BF16))
    rank1 = jnp.sum(jnp.where(oh1, prefix, 0.0), axis=0, keepdims=True) - 1.0
    rank2 = jnp.sum(jnp.where(oh2, prefix, 0.0), axis=0, keepdims=True) - 1.0
    route_ref[R_E1:R_E1 + 1, :] = e1
    route_ref[R_E2:R_E2 + 1, :] = e2
    route_ref[R_RANK1:R_RANK1 + 1, :] = rank1
    route_ref[R_RANK2:R_RANK2 + 1, :] = rank2
    route_ref[R_CW1:R_CW1 + 1, :] = cw1
    route_ref[R_CW2:R_CW2 + 1, :] = cw2
    route_ref[R_CW2 + 1:ROUTE_ROWS, :] = jnp.zeros((ROUTE_ROWS - R_CW2 - 1, tm), F32)
    cnt_ref[0] = jnp.broadcast_to(jnp.sum(onehot, axis=1, keepdims=True), (N_EXPERTS, LANES))


def _merge(x2d, cconv, omla, omem, consts):
    m, d = x2d.shape
    tm = TOKEN_TILE
    nt = m // tm
    row = lambda w: pl.BlockSpec((tm, w), lambda i: (i, 0))
    sds = jax.ShapeDtypeStruct
    return pl.pallas_call(
        _merge_kernel,
        out_shape=(sds((m, d), F32), sds((m, d), BF16), sds((ROUTE_ROWS, m), F32),
                   sds((nt, N_EXPERTS, LANES), F32)),
        grid=(nt,),
        in_specs=[row(d), row(cconv.shape[1]), row(omla.shape[1]), row(omem.shape[1])] + [_full(c) for c in consts],
        out_specs=(row(d), row(d), pl.BlockSpec((ROUTE_ROWS, tm), lambda i: (0, i)),
                   pl.BlockSpec((1, N_EXPERTS, LANES), lambda i: (i, 0, 0))),
        compiler_params=_params(),
        name="merge",
    )(x2d, cconv, omla, omem, *consts)


def _local_positions(route_ref, lsb_ref):
    tm = route_ref.shape[1]
    lsb = lsb_ref[0]
    eio = lax.broadcasted_iota(I32, (N_EXPERTS, tm), 0).astype(F32)
    pos = []
    for r_e, r_rank in ((R_E1, R_RANK1), (R_E2, R_RANK2)):
        e = route_ref[r_e:r_e + 1, :]
        start = jnp.sum(jnp.where(eio == e, lsb, 0.0), axis=0, keepdims=True)
        pos.append((start + route_ref[r_rank:r_rank + 1, :]).astype(I32))
    return pos


def _segment_copies(i, ls_ref, nch_ref, gd_ref, make_copy):
    def seg(e, carry):
        k = i * N_EXPERTS + e
        ls = ls_ref[k]
        gd = gd_ref[k]

        def chunk(c, carry2):
            make_copy(pl.multiple_of(ls + c * ROW_CHUNK, ROW_CHUNK),
                      pl.multiple_of(gd + c * ROW_CHUNK, ROW_CHUNK)).start()
            return carry2

        return lax.fori_loop(0, nch_ref[k], chunk, carry)

    lax.fori_loop(0, N_EXPERTS, seg, 0)


def _zero_fill_copies(ztab_ref, zero_scr, buf_ref, zsem, wait):
    def run(rows, start_of, count_of, n_groups):
        def group(g, carry):
            start = start_of(g)

            def one(c, carry2):
                cp = pltpu.make_async_copy(zero_scr.at[pl.ds(0, rows)],
                                           buf_ref.at[pl.ds(pl.multiple_of(start + c * rows, ROW_CHUNK), rows)], zsem)
                cp.wait() if wait else cp.start()
                return carry2

            return lax.fori_loop(0, count_of(g), one, carry)

        lax.fori_loop(0, n_groups, group, 0)

    n = N_EXPERTS
    run(ZERO_MID, lambda e: ztab_ref[e], lambda e: ztab_ref[n + e], n)
    run(ROW_CHUNK, lambda e: ztab_ref[2 * n + e], lambda e: ztab_ref[3 * n + e], n)
    run(EXPERT_TILE, lambda g: ztab_ref[4 * n], lambda g: ztab_ref[4 * n + 1], 1)


def _dispatch_kernel(nt_p, ls_ref, nch_ref, gd_ref, ntot_ref, ztab_ref, xp_ref, xs_ref, route_ref, lsb_ref,
                     buf_ref, xs_scr, zero_scr, sem, zsem):
    i = pl.program_id(0)
    last = pl.num_programs(0) - 1
    slot = i % 2
    tm = xp_ref.shape[0]

    @pl.when(i == 0)
    def _():
        zero_scr[...] = jnp.zeros(zero_scr.shape, F32)
        _zero_fill_copies(ztab_ref, zero_scr, buf_ref, zsem, wait=False)

    lp1, lp2 = _local_positions(route_ref, lsb_ref)
    rowi = lax.broadcasted_iota(I32, (SORT_ROWS, tm), 0)
    sel1 = rowi == lp1
    sel2 = rowi == lp2
    perm = jnp.where(sel1, 1.0, jnp.where(sel2, 1.0, 0.0)).astype(BF16)

    d = xp_ref.shape[1]

    def sort_rows(x_ref):
        xs_scr[slot, :, 0:d] = _dot(perm, x_ref[...])

    pl.when(i < nt_p)(lambda: sort_rows(xp_ref))
    pl.when(i >= nt_p)(lambda: sort_rows(xs_ref))
    cw = jnp.sum(jnp.where(sel1, route_ref[R_CW1:R_CW1 + 1, :],
                           jnp.where(sel2, route_ref[R_CW2:R_CW2 + 1, :], 0.0)), axis=1, keepdims=True)
    xs_scr[slot, :, d:d + CW_LANES] = jnp.broadcast_to(cw, (SORT_ROWS, CW_LANES))

    def make_copy(s):
        def f(src_row, dst_row):
            return pltpu.make_async_copy(xs_scr.at[s, pl.ds(src_row, ROW_CHUNK)],
                                         buf_ref.at[pl.ds(dst_row, ROW_CHUNK)], sem.at[s])
        return f

    def drain(step, s):
        def body(c, carry):
            make_copy(s)(0, 0).wait()
            return carry
        lax.fori_loop(0, ntot_ref[step], body, 0)

    _segment_copies(i, ls_ref, nch_ref, gd_ref, make_copy(slot))
    pl.when(i > 0)(lambda: drain(i - 1, 1 - slot))

    @pl.when(i == last)
    def _():
        drain(i, slot)
        _zero_fill_copies(ztab_ref, zero_scr, buf_ref, zsem, wait=True)


def _dispatch(seg, ztab, x1b_p, x1b_s, route, lsb, buf_rows):
    d = x1b_p.shape[1]
    tm = TOKEN_TILE
    nt_p, nt_s = x1b_p.shape[0] // tm, x1b_s.shape[0] // tm
    width = d + CW_LANES
    grid_spec = pltpu.PrefetchScalarGridSpec(
        num_scalar_prefetch=5,
        grid=(nt_p + nt_s,),
        in_specs=[pl.BlockSpec((tm, d), lambda i, *_: (jnp.minimum(i, nt_p - 1), 0)),
                  pl.BlockSpec((tm, d), lambda i, *_: (jnp.maximum(i - nt_p, 0), 0)),
                  pl.BlockSpec((ROUTE_ROWS, tm), lambda i, *_: (0, i)),
                  pl.BlockSpec((1, N_EXPERTS, tm), lambda i, *_: (i, 0, 0))],
        out_specs=pl.BlockSpec(memory_space=pl.ANY),
        scratch_shapes=[pltpu.VMEM((2, SORT_ROWS, width), F32), pltpu.VMEM((EXPERT_TILE, width), F32),
                        pltpu.SemaphoreType.DMA((2,)), pltpu.SemaphoreType.DMA(())],
    )
    return pl.pallas_call(
        functools.partial(_dispatch_kernel, nt_p),
        out_shape=jax.ShapeDtypeStruct((buf_rows, width), F32),
        grid_spec=grid_spec,
        compiler_params=_params(),
        name="dispatch",
    )(*seg, ztab, x1b_p, x1b_s, route, lsb)


def _experts_kernel(te_ref, nu_ref, xs_ref, w1_ref, w3_ref, w2_ref, y_ref):
    @pl.when(pl.program_id(0) >= nu_ref[0])
    def _():
        y_ref[...] = jnp.zeros(y_ref.shape, F32)

    @pl.when(pl.program_id(0) < nu_ref[0])
    def _():
        d = y_ref.shape[1]
        x = xs_ref[:, 0:d].astype(BF16)
        cw = xs_ref[:, d:d + 1]
        h1 = _dot(x, w1_ref[0].astype(BF16))
        h3 = _dot(x, w3_ref[0].astype(BF16))
        h = (h1 * _sigmoid(h1) * h3 * cw).astype(BF16)
        y_ref[...] = _dot(h, w2_ref[0].astype(BF16))


def _experts(te, nu, buf, w1, w3, w2):
    rows, width = buf.shape
    tr = EXPERT_TILE
    clamp = lambda i, te_ref, nu_ref: (jnp.minimum(i, nu_ref[0] - 1), 0)
    wspec = lambda w: pl.BlockSpec((1,) + w.shape[1:], lambda i, te_ref, nu_ref: (te_ref[i], 0, 0))
    grid_spec = pltpu.PrefetchScalarGridSpec(
        num_scalar_prefetch=2,
        grid=(rows // tr,),
        in_specs=[pl.BlockSpec((tr, width), clamp), wspec(w1), wspec(w3), wspec(w2)],
        out_specs=pl.BlockSpec((tr, width - CW_LANES), lambda i, te_ref, nu_ref: (i, 0)),
    )
    return pl.pallas_call(
        _experts_kernel,
        out_shape=jax.ShapeDtypeStruct((rows, width - CW_LANES), F32),
        grid_spec=grid_spec,
        compiler_params=_params(),
        name="experts",
    )(te, nu, buf, w1, w3, w2)


def _combine_kernel(ls_ref, nch_ref, gd_ref, ntot_ref, y_hbm, route_ref, lsb_ref, x1_ref, g2_ref, b2_ref,
                    out_ref, ys_scr, sem):
    i = pl.program_id(0)
    last = pl.num_programs(0) - 1
    slot = i % 2
    tm = x1_ref.shape[0]

    def make_copy(s):
        def f(src_row, dst_row):
            return pltpu.make_async_copy(y_hbm.at[pl.ds(dst_row, ROW_CHUNK)],
                                         ys_scr.at[s, pl.ds(src_row, ROW_CHUNK)], sem.at[s])
        return f

    def fetch(step, s):
        ys_scr[s] = jnp.zeros(ys_scr.shape[1:], F32)
        _segment_copies(step, ls_ref, nch_ref, gd_ref, make_copy(s))

    pl.when(i == 0)(lambda: fetch(i, slot))
    pl.when(i < last)(lambda: fetch(i + 1, 1 - slot))
    lp1, lp2 = _local_positions(route_ref, lsb_ref)
    rowi = lax.broadcasted_iota(I32, (SORT_ROWS, tm), 0)
    perm = jnp.where(rowi == lp1, 1.0, jnp.where(rowi == lp2, 1.0, 0.0)).astype(BF16)

    def drain(c, carry):
        make_copy(slot)(0, 0).wait()
        return carry

    lax.fori_loop(0, ntot_ref[i], drain, 0)
    y = _dot_tn(perm, ys_scr[slot].astype(BF16))
    out_ref[...] = _layer_norm(DEEPNORM_ALPHA * x1_ref[...] + y, g2_ref[...], b2_ref[...])


def _combine(seg, ybuf, route, lsb, x1, g2, b2):
    m, d = x1.shape
    tm = TOKEN_TILE
    nt = m // tm
    grid_spec = pltpu.PrefetchScalarGridSpec(
        num_scalar_prefetch=4,
        grid=(nt,),
        in_specs=[pl.BlockSpec(memory_space=pl.ANY),
                  pl.BlockSpec((ROUTE_ROWS, tm), lambda i, *_: (0, i)),
                  pl.BlockSpec((1, N_EXPERTS, tm), lambda i, *_: (i, 0, 0)),
                  pl.BlockSpec((tm, d), lambda i, *_: (i, 0)),
                  pl.BlockSpec(g2.shape, lambda i, *_: (0, 0)),
                  pl.BlockSpec(b2.shape, lambda i, *_: (0, 0))],
        out_specs=pl.BlockSpec((tm, d), lambda i, *_: (i, 0)),
        scratch_shapes=[pltpu.VMEM((2, SORT_ROWS, d), F32), pltpu.SemaphoreType.DMA((2,))],
    )
    return pl.pallas_call(
        _combine_kernel,
        out_shape=jax.ShapeDtypeStruct((m, d), F32),
        grid_spec=grid_spec,
        compiler_params=_params(),
        name="combine",
    )(*seg, ybuf, route, lsb, x1, g2, b2)


def _rope_tables(pos):
    half = QK_ROPE // 2
    inv = jnp.power(ROPE_BASE, -jnp.arange(half, dtype=F32) / half)
    ang = pos.astype(F32)[:, None] * inv[None, :]
    cos, sin = jnp.cos(ang), jnp.sin(ang)
    ck = jnp.concatenate([cos, cos], axis=1)
    sk = jnp.concatenate([-sin, sin], axis=1)
    cq = jnp.tile(ck, (1, MLA_HEADS)) * MLA_SCALE
    sq = jnp.tile(sk, (1, MLA_HEADS)) * MLA_SCALE
    return ck, sk, cq, sq


def _rope_tables_padded(pos):
    ck, sk, _, _ = _rope_tables(pos)
    n = pos.shape[0]
    s2 = MLA_SCALE * 1.4426950408889634
    z = lambda w: jnp.zeros((n, w), F32)
    tail = HEAD_PAD - QK_NOPE - QK_ROPE
    ca = jnp.concatenate([jnp.full((n, QK_NOPE), s2, F32), ck * s2, z(tail)], axis=1)
    sb = jnp.concatenate([z(QK_NOPE), sk * s2, z(tail)], axis=1)
    ckp = jnp.concatenate([z(QK_NOPE), ck, z(tail)], axis=1)
    skp = jnp.concatenate([z(QK_NOPE), sk, z(tail)], axis=1)
    return ca, sb, ckp, skp


def _ceil_to(x, k):
    return (x + k - 1) // k * k


def _segment_tables(cnt, n_row_tiles):
    cnt8 = _ceil_to(cnt, ROW_CHUNK)
    lstart = jnp.cumsum(cnt8, axis=1) - cnt8
    filled = jnp.sum(cnt8, axis=0)
    tot = _ceil_to(filled, EXPERT_TILE)
    ends = jnp.cumsum(tot)
    gdst = (ends - tot)[None, :] + jnp.cumsum(cnt8, axis=0) - cnt8
    nch = cnt8 // ROW_CHUNK
    n_used = ends[-1] // EXPERT_TILE
    tile_row = jnp.minimum(jnp.arange(n_row_tiles, dtype=I32), n_used - 1) * EXPERT_TILE
    tile_expert = jnp.minimum(jnp.sum((ends[None, :] <= tile_row[:, None]).astype(I32), axis=1), N_EXPERTS - 1)
    tail0 = ends - tot + filled
    tail = tot - filled
    n_mid = tail // ZERO_MID
    ztab = jnp.concatenate([tail0, n_mid, tail0 + n_mid * ZERO_MID, (tail % ZERO_MID) // ROW_CHUNK,
                            ends[-1:], (n_row_tiles - n_used).reshape(1)]).astype(I32)
    return lstart, gdst, nch, jnp.sum(nch, axis=1), tile_expert, n_used.reshape(1).astype(I32), ztab


def kernel(x_prompt, x_sample, cache_conv, cache_mla_latent, cache_mla_krope, cache_mem_k, cache_mem_v,
           mem_prompt, w_in, b_gate, w_dw, b_dw, g_cn, b_cn, w_co, g_qa, w_uq, g_kva, w_uk, w_uv, w_mo,
           w_mem_k, w_mem_v, w_mem_o, w_out, g_ln1, b_ln1, w_group, b_group, w_router, b_router,
           w_e1, w_e3, w_e2, g_ln2, b_ln2):
    n_p, s_p, d = x_prompt.shape
    n_s, s_s, _ = x_sample.shape
    past = cache_mla_latent.shape[2]
    n_mem = mem_prompt.shape[1]
    assert w_in.shape[0] == DEPTH == 1
    assert s_p % TOKEN_TILE == 0 and s_p % ATT_TQ == 0 and TOKEN_TILE % s_s == 0 and (n_s * s_s) % TOKEN_TILE == 0
    assert s_s <= CHUNK and past % CHUNK == 0 and past % SAMPLE_KV == 0
    assert n_s % SAMPLE_SEQS == 0
    mem_w = MEM_HEADS * MEM_HEAD_DIM
    row2 = lambda v: v.reshape(1, -1)

    wl = w_in[0]
    sp = (C_GLU, C_GLU + Q_LORA, C_GLU + Q_LORA + KV_LORA + QK_ROPE, C_GLU + Q_LORA + KV_LORA + QK_ROPE + mem_w)
    w_glu, w_qd, w_kvd, w_mq, w_gate = wl[:, :sp[0]], wl[:, sp[0]:sp[1]], wl[:, sp[1]:sp[2]], wl[:, sp[2]:sp[3]], wl[:, sp[3]:]
    half = QK_ROPE // 2
    w_kr = w_kvd[:, KV_LORA:]
    w_krs = jnp.concatenate([w_kr[:, half:], w_kr[:, :half]], axis=1)
    pad = jnp.zeros((d, C_MQ - (C_KRS + QK_ROPE)), F32)
    w_a = jnp.concatenate([w_glu, w_qd, w_kvd, w_krs, pad, w_mq], axis=1).astype(BF16)
    wq = w_uq[0].reshape(Q_LORA, MLA_HEADS, QK_NOPE + QK_ROPE)
    wq_r = wq[:, :, QK_NOPE:]
    wq_rs = jnp.concatenate([wq_r[:, :, half:], wq_r[:, :, :half]], axis=2)
    w_uq2 = jnp.concatenate([wq[:, :, :QK_NOPE].reshape(Q_LORA, -1), wq_r.reshape(Q_LORA, -1),
                             wq_rs.reshape(Q_LORA, -1)], axis=1).astype(BF16)
    wdw8 = jnp.broadcast_to(w_dw[0][:, None, :], (CONV_WIDTH, SUBLANES, CONV_CH))
    in_consts = (w_a, w_uq2, wdw8, row2(b_dw[0]), row2(g_cn[0]), row2(b_cn[0]), row2(g_qa[0]), row2(g_kva[0]))
    wuk_t = jnp.transpose(w_uk[0], (1, 2, 0)).astype(BF16)
    wuv_h = jnp.transpose(w_uv[0], (1, 0, 2)).astype(BF16)
    w_r = jnp.zeros((ROUTER_ROWS, d), F32).at[:N_EXPERTS].set(w_router[0].T).at[R_GROUP0:R_GROUP0 + N_GROUPS].set(w_group[0].T)
    b_r = jnp.zeros((ROUTER_ROWS, 1), F32).at[:N_EXPERTS, 0].set(b_router[0]).at[R_GROUP0:R_GROUP0 + N_GROUPS, 0].set(b_group[0])
    merge_consts = (w_gate.astype(BF16), row2(b_gate[0]), w_co[0].astype(BF16), w_mo[0].astype(BF16),
                    w_mem_o[0].astype(BF16), w_out[0].astype(BF16), row2(g_ln1[0]), row2(b_ln1[0]), w_r, b_r)

    zc = lambda n: jnp.zeros((d, n), F32)
    lane_pad = HEAD_PAD - QK_NOPE - QK_ROPE
    w_a_p = jnp.concatenate([w_glu, w_qd, w_kvd[:, :KV_LORA], zc(QK_NOPE), w_kr, zc(lane_pad),
                             zc(QK_NOPE), w_krs, zc(lane_pad), w_mq], axis=1).astype(BF16)
    zq = lambda n: jnp.zeros((Q_LORA, MLA_HEADS, n), F32)
    w_qa = jnp.concatenate([wq, zq(lane_pad)], axis=2).reshape(Q_LORA, QK_PAD_W).astype(BF16)
    w_qb = jnp.concatenate([zq(QK_NOPE), wq_rs, zq(lane_pad)], axis=2).reshape(Q_LORA, QK_PAD_W).astype(BF16)
    w_ukp = jnp.concatenate([w_uk[0], jnp.zeros((KV_LORA, MLA_HEADS, HEAD_PAD - QK_NOPE), F32)],
                            axis=2).reshape(KV_LORA, QK_PAD_W).astype(BF16)
    w_uvt = w_uv[0].reshape(KV_LORA, V_ALL).T.astype(BF16)
    in_consts_p = (w_a_p, w_qa, w_qb, w_ukp, w_uvt) + in_consts[2:]

    xp = x_prompt.reshape(n_p * s_p, d)
    mk_p, mv_p = _mem_proj(mem_prompt, w_mem_k[0].astype(BF16), w_mem_v[0].astype(BF16))
    (cc_p, nconv_p, ckv_p, krot_p, qp_p, kp_p, vt_p, mq_p) = _inproj(
        xp, None, in_consts_p, _rope_tables_padded(jnp.arange(s_p, dtype=I32)),
        nseg=1, seg_len=TOKEN_TILE, tiles_per_seq=s_p // TOKEN_TILE, nseq=n_p)
    omla_p, omem_p = _attn_prompt(qp_p, mq_p, kp_p, vt_p, mk_p, mv_p, nseq=n_p, seq=s_p)
    x1_p, x1b_p, route_p, cnt_p = _merge(xp, cc_p, omla_p, omem_p, merge_consts)

    xs = x_sample.reshape(n_s * s_s, d)
    seqs_per_tile = TOKEN_TILE // s_s
    tabs_s = tuple(jnp.tile(t, (seqs_per_tile, 1)) for t in _rope_tables(past + jnp.arange(s_s, dtype=I32)))
    (cc_s, nconv_s, ckv_s, kcb_s, krot_s, krb_s, qn_s, qr_s, mq_s) = _inproj(
        xs, cache_conv[0], in_consts, tabs_s, nseg=seqs_per_tile, seg_len=s_s, tiles_per_seq=1, nseq=n_s)
    omla_s, omem_s = _attn_sample(qn_s, qr_s, mq_s, cache_mla_latent[0], cache_mla_krope[0], kcb_s, krb_s,
                                  cache_mem_k[0].reshape(n_s, n_mem * MEM_HEADS, MEM_HEAD_DIM),
                                  cache_mem_v[0].reshape(n_s, n_mem * MEM_HEADS, MEM_HEAD_DIM),
                                  wuk_t, wuv_h, nseq=n_s, seq=s_s)
    x1_s, x1b_s, route_s, cnt_s = _merge(xs, cc_s, omla_s, omem_s, merge_consts)

    nt_p, nt_s = cnt_p.shape[0], cnt_s.shape[0]
    cnt = jnp.concatenate([cnt_p[:, :, 0], cnt_s[:, :, 0]], axis=0).astype(I32)
    m_tot = (nt_p + nt_s) * TOKEN_TILE
    buf_rows = _ceil_to(2 * m_tot + (nt_p + nt_s) * N_EXPERTS * (ROW_CHUNK - 1) + N_EXPERTS * (EXPERT_TILE - 1), EXPERT_TILE)
    lstart, gdst, nch, ntot, tile_expert, n_used, ztab = _segment_tables(cnt, buf_rows // EXPERT_TILE)
    lsb = jnp.broadcast_to(lstart.astype(F32)[:, :, None], lstart.shape + (TOKEN_TILE,))
    seg = lambda a, b: (lstart[a:b].reshape(-1), nch[a:b].reshape(-1), gdst[a:b].reshape(-1), ntot[a:b])
    seg_p, seg_s = seg(0, nt_p), seg(nt_p, nt_p + nt_s)
    buf = _dispatch(seg(0, nt_p + nt_s), ztab, x1b_p, x1b_s, jnp.concatenate([route_p, route_s], axis=1), lsb, buf_rows)
    n_e = N_EXPERTS
    ybuf = _experts(tile_expert, n_used, buf, w_e1[0].reshape((n_e,) + w_e1.shape[3:]),
                    w_e3[0].reshape((n_e,) + w_e3.shape[3:]), w_e2[0].reshape((n_e,) + w_e2.shape[3:]))
    g2, b2 = row2(g_ln2[0]), row2(b_ln2[0])
    y_p = _combine(seg_p, ybuf, route_p, lsb[:nt_p], x1_p, g2, b2)
    y_s = _combine(seg_s, ybuf, route_s, lsb[nt_p:], x1_s, g2, b2)

    lead = lambda a, n, t: a.reshape((1, n, t) + a.shape[1:])
    return (y_p.reshape(n_p, s_p, d), y_s.reshape(n_s, s_s, d),
            nconv_p[None], lead(ckv_p, n_p, s_p), lead(krot_p, n_p, s_p),
            mk_p.reshape(1, n_p, n_mem, MEM_HEADS, MEM_HEAD_DIM), mv_p.reshape(1, n_p, n_mem, MEM_HEADS, MEM_HEAD_DIM),
            nconv_s[None], lead(ckv_s, n_s, s_s), lead(krot_s, n_s, s_s))
```

```python
import functools

import jax
import jax.numpy as jnp
from jax import lax
from jax.experimental import pallas as pl
from jax.experimental.pallas import tpu as pltpu

F32 = jnp.float32
BF16 = jnp.bfloat16
U32 = jnp.uint32
I32 = jnp.int32

DEPTH = 1
CHUNK = 64
EPS = 1e-5
CONV_CH = 512
CONV_WIDTH = 31
HALO = CONV_WIDTH - 1
MLA_HEADS = 8
Q_LORA = 384
KV_LORA = 256
QK_NOPE = 64
QK_ROPE = 32
V_HEAD = 64
ROPE_BASE = 10000.0
MLA_SCALE = (QK_NOPE + QK_ROPE) ** -0.5
MEM_HEADS = 4
MEM_HEAD_DIM = 128
MEM_SCALE = MEM_HEAD_DIM ** -0.5
N_GROUPS = 4
EXPERTS_PER_GROUP = 8
N_EXPERTS = N_GROUPS * EXPERTS_PER_GROUP
DEEPNORM_ALPHA = (2.0 * DEPTH) ** 0.25

SUBLANES = 8
LANES = 128
TOKEN_TILE = 512
HALO_ROWS = 32
CONV_ROWS = 32
ATT_TQ = 256
SAMPLE_KV = 512
SAMPLE_SEQS = 2
ROW_CHUNK = 2 * SUBLANES
EXPERT_TILE = 512
ZERO_MID = 64
SORT_ROWS = 2 * TOKEN_TILE + N_EXPERTS * ROW_CHUNK
CW_LANES = LANES
CW_PARTS = 3
VMEM_LIMIT = 56 * 1024 * 1024


def _dot(a, b):
    return jnp.dot(a, b, preferred_element_type=F32)


def _dot_nt(a, b):
    return lax.dot_general(a, b, (((1,), (1,)), ((), ())), preferred_element_type=F32)


def _dot_tn(a, b):
    return lax.dot_general(a, b, (((0,), (0,)), ((), ())), preferred_element_type=F32)


def _sigmoid(x):
    return 1.0 / (1.0 + jnp.exp(-x))


def _layer_norm(x, g, b):
    mu = jnp.mean(x, axis=-1, keepdims=True)
    d = x - mu
    var = jnp.mean(d * d, axis=-1, keepdims=True)
    return d * lax.rsqrt(var + EPS) * g + b


def _rms_norm(x, g):
    return x * lax.rsqrt(jnp.mean(x * x, axis=-1, keepdims=True) + EPS) * g


def _full(a):
    nd = a.ndim
    return pl.BlockSpec(a.shape, lambda *_: (0,) * nd)


def _params(n_axes=1, vmem=VMEM_LIMIT):
    return pltpu.CompilerParams(dimension_semantics=("arbitrary",) * n_axes, vmem_limit_bytes=vmem)


def _head_rows(h, n_tok):
    return pl.ds(h, n_tok, stride=MEM_HEADS)


def _mem_proj_kernel(m_ref, wk_ref, wv_ref, k_ref, v_ref):
    m = m_ref[0].astype(BF16)
    t = m.shape[0]
    k = _dot(m, wk_ref[...])
    v = _dot(m, wv_ref[...])
    for h in range(MEM_HEADS):
        cols = slice(h * MEM_HEAD_DIM, (h + 1) * MEM_HEAD_DIM)
        k_ref.at[0][_head_rows(h, t), :] = k[:, cols]
        v_ref.at[0][_head_rows(h, t), :] = v[:, cols]


def _mem_proj(mem, wk, wv):
    n, t, d = mem.shape
    out = jax.ShapeDtypeStruct((n, t * MEM_HEADS, MEM_HEAD_DIM), F32)
    return pl.pallas_call(
        _mem_proj_kernel,
        out_shape=(out, out),
        grid=(n,),
        in_specs=[pl.BlockSpec((1, t, d), lambda i: (i, 0, 0)), _full(wk), _full(wv)],
        out_specs=(pl.BlockSpec((1, t * MEM_HEADS, MEM_HEAD_DIM), lambda i: (i, 0, 0)),) * 2,
        compiler_params=_params(),
        name="mem_proj",
    )(mem, wk, wv)


C_GLU = 2 * CONV_CH
C_QD = C_GLU
C_KV = C_QD + Q_LORA
C_KR = C_KV + KV_LORA
C_KRS = C_KR + QK_ROPE
C_MQ = 1792
C_END = C_MQ + MEM_HEADS * MEM_HEAD_DIM
Q_NOPE_W = MLA_HEADS * QK_NOPE
Q_ROPE_W = MLA_HEADS * QK_ROPE
HEAD_PAD = LANES
P_KRB = C_KR
P_KRSB = P_KRB + HEAD_PAD
P_MQ = P_KRSB + HEAD_PAD
P_END = P_MQ + MEM_HEADS * MEM_HEAD_DIM
QK_PAD_W = MLA_HEADS * HEAD_PAD
V_ALL = MLA_HEADS * V_HEAD


def _inproj_kernel(nseg, seg_len, tiles_per_seq, has_state, *refs):
    it = iter(refs)
    x_ref = next(it)
    st_ref = next(it) if has_state else None
    if has_state:
        (w_ref, wuq_ref, wdw_ref, bdw_ref, gcn_ref, bcn_ref, gqa_ref, gkva_ref,
         ck_ref, sk_ref, cq_ref, sq_ref,
         cconv_ref, nconv_ref, ckv_ref, kcb_ref, krot_ref, krb_ref, qn_ref, qr_ref, mq_ref,
         win) = it
    else:
        (w_ref, wqa_ref, wqb_ref, wukp_ref, wuvt_ref, wdw_ref, bdw_ref, gcn_ref, bcn_ref, gqa_ref, gkva_ref,
         ca_ref, sb_ref, ckp_ref, skp_ref,
         cconv_ref, nconv_ref, ckv_ref, krot_ref, qp_ref, kp_ref, vt_ref, mq_ref,
         win) = it

    xb = x_ref[...].astype(BF16)
    ab = _dot(xb, w_ref[:, 0:C_GLU])
    u = ab[:, :CONV_CH] * _sigmoid(ab[:, CONV_CH:])

    if has_state:
        for s in range(nseg):
            win[s, HALO_ROWS - HALO:HALO_ROWS, :] = st_ref[s]
            win[s, HALO_ROWS:HALO_ROWS + seg_len, :] = u[s * seg_len:(s + 1) * seg_len]
    else:
        j = pl.program_id(0) % tiles_per_seq

        @pl.when(j == 0)
        def _():
            win[0, 0:HALO_ROWS, :] = jnp.zeros((HALO_ROWS, CONV_CH), F32)

        @pl.when(j != 0)
        def _():
            win[0, 0:HALO_ROWS, :] = win[0, seg_len:seg_len + HALO_ROWS, :]

        win[0, HALO_ROWS:HALO_ROWS + seg_len, :] = u

    g_cn = gcn_ref[...]
    b_cn = bcn_ref[...]
    b_dw = bdw_ref[...]
    base = HALO_ROWS - HALO
    span = HALO_ROWS + CONV_ROWS
    for s in range(nseg):
        for rc in range(seg_len // CONV_ROWS):
            r0 = rc * CONV_ROWS
            xw = win[s, r0:r0 + span, :]
            acc = jnp.zeros((CONV_ROWS // SUBLANES, SUBLANES, CONV_CH), F32) + b_dw
            for b in range(SUBLANES):
                xsh = xw if b == 0 else xw[b:b + span - SUBLANES, :]
                for a in range(span // SUBLANES):
                    k = a * SUBLANES + b - base
                    if 0 <= k < CONV_WIDTH:
                        xk = xsh[a * SUBLANES:a * SUBLANES + CONV_ROWS, :]
                        acc = acc + xk.reshape(CONV_ROWS // SUBLANES, SUBLANES, CONV_CH) * wdw_ref[k]
            y = _layer_norm(acc.reshape(CONV_ROWS, CONV_CH), g_cn, b_cn)
            y = y * _sigmoid(y)
            cconv_ref[s * seg_len + r0:s * seg_len + r0 + CONV_ROWS, :] = y.astype(BF16)
        nconv_ref[s] = win[s, seg_len + HALO_ROWS - HALO:seg_len + HALO_ROWS, :]

    rest = _dot(xb, w_ref[:, C_QD:w_ref.shape[1]])
    o = -C_QD
    qd = rest[:, C_QD + o:C_KV + o]
    qn = _rms_norm(qd, gqa_ref[...]).astype(BF16)
    if not has_state:
        qp = (_dot(qn, wqa_ref[...]) * _rep(ca_ref[...], MLA_HEADS)
              + _dot(qn, wqb_ref[...]) * _rep(sb_ref[...], MLA_HEADS))
        qp_ref[...] = qp.astype(BF16)
        ckv = _rms_norm(rest[:, C_KV + o:C_KR + o], gkva_ref[...])
        ckv_ref[...] = ckv
        cb = ckv.astype(BF16)
        kr = (rest[:, P_KRB + o:P_KRB + HEAD_PAD + o] * ckp_ref[...]
              + rest[:, P_KRSB + o:P_KRSB + HEAD_PAD + o] * skp_ref[...])
        krot_ref[...] = kr[:, QK_NOPE:QK_NOPE + QK_ROPE]
        kp_ref[...] = (_dot(cb, wukp_ref[...]) + _rep(kr, MLA_HEADS)).astype(BF16)
        vt = _dot_nt(wuvt_ref[...], cb).astype(BF16)
        for c in range(vt_ref.shape[1]):
            vt_ref[0, c] = vt[:, c * ATT_TQ:(c + 1) * ATT_TQ]
        mq_ref[...] = rest[:, P_MQ + o:P_END + o].astype(BF16)
        return
    q = _dot(qn, wuq_ref[...])
    qn_ref[...] = (q[:, :Q_NOPE_W] * MLA_SCALE).astype(BF16)
    qr_ref[...] = (q[:, Q_NOPE_W:Q_NOPE_W + Q_ROPE_W] * cq_ref[...]
                   + q[:, Q_NOPE_W + Q_ROPE_W:] * sq_ref[...]).astype(BF16)
    ckv = _rms_norm(rest[:, C_KV + o:C_KR + o], gkva_ref[...])
    ckv_ref[...] = ckv
    kcb_ref[...] = ckv.astype(BF16)
    kr = rest[:, C_KR + o:C_KRS + o] * ck_ref[...] + rest[:, C_KRS + o:C_KRS + QK_ROPE + o] * sk_ref[...]
    krot_ref[...] = kr
    krb_ref[...] = kr.astype(BF16)
    mq_ref[...] = rest[:, C_MQ + o:C_END + o].astype(BF16)


def _inproj(x2d, state, consts, tables, *, nseg, seg_len, tiles_per_seq, nseq):
    m, d = x2d.shape
    tm = nseg * seg_len
    nt = m // tm
    has_state = state is not None
    tbl_tiles = tables[0].shape[0] // tm
    in_specs = [pl.BlockSpec((tm, d), lambda i: (i, 0))]
    args = [x2d]
    if has_state:
        in_specs.append(pl.BlockSpec((nseg, HALO, CONV_CH), lambda i: (i, 0, 0)))
        args.append(state)
    in_specs += [_full(c) for c in consts]
    args += list(consts)
    for t in tables:
        in_specs.append(pl.BlockSpec((tm, t.shape[1]), lambda i: (i % tbl_tiles, 0)))
        args.append(t)
    row = lambda w: pl.BlockSpec((tm, w), lambda i: (i, 0))
    if has_state:
        nconv_spec = pl.BlockSpec((nseg, HALO, CONV_CH), lambda i: (i, 0, 0))
    else:
        nconv_spec = pl.BlockSpec((1, HALO, CONV_CH), lambda i: (i // tiles_per_seq, 0, 0))
    sds = jax.ShapeDtypeStruct
    mem_w = MEM_HEADS * MEM_HEAD_DIM
    if has_state:
        out_shape = (sds((m, CONV_CH), BF16), sds((nseq, HALO, CONV_CH), F32), sds((m, KV_LORA), F32),
                     sds((m, KV_LORA), BF16), sds((m, QK_ROPE), F32), sds((m, QK_ROPE), BF16),
                     sds((m, Q_NOPE_W), BF16), sds((m, Q_ROPE_W), BF16), sds((m, mem_w), BF16))
        out_specs = (row(CONV_CH), nconv_spec, row(KV_LORA), row(KV_LORA), row(QK_ROPE), row(QK_ROPE),
                     row(Q_NOPE_W), row(Q_ROPE_W), row(mem_w))
    else:
        kv_tiles = tm // ATT_TQ
        out_shape = (sds((m, CONV_CH), BF16), sds((nseq, HALO, CONV_CH), F32), sds((m, KV_LORA), F32),
                     sds((m, QK_ROPE), F32), sds((m, QK_PAD_W), BF16), sds((m, QK_PAD_W), BF16),
                     sds((nseq, tiles_per_seq * kv_tiles, V_ALL, ATT_TQ), BF16), sds((m, mem_w), BF16))
        out_specs = (row(CONV_CH), nconv_spec, row(KV_LORA), row(QK_ROPE), row(QK_PAD_W), row(QK_PAD_W),
                     pl.BlockSpec((1, kv_tiles, V_ALL, ATT_TQ),
                                  lambda i: (i // tiles_per_seq, i % tiles_per_seq, 0, 0)),
                     row(mem_w))
    return pl.pallas_call(
        functools.partial(_inproj_kernel, nseg, seg_len, tiles_per_seq, has_state),
        out_shape=out_shape,
        grid=(nt,),
        in_specs=in_specs,
        out_specs=out_specs,
        scratch_shapes=[pltpu.VMEM((nseg, HALO_ROWS + seg_len, CONV_CH), F32)],
        compiler_params=_params(),
        name="inproj_prompt" if not has_state else "inproj_sample",
    )(*args)


def _rep(x, n):
    return x if n == 1 else jnp.concatenate([x] * n, axis=1)


def _mem_attention(mq_ref, mk_ref, mv_ref, o_ref):
    nsq = mk_ref.shape[0]
    tq = mq_ref.shape[0] // nsq
    n_tok = mk_ref.shape[1] // MEM_HEADS
    cols = [slice(h * MEM_HEAD_DIM, (h + 1) * MEM_HEAD_DIM) for h in range(MEM_HEADS)]
    pairs = [(n, h) for n in range(nsq) for h in range(MEM_HEADS)]
    rows = lambda n: slice(n * tq, (n + 1) * tq)
    scores = [_dot_nt(mq_ref[rows(n), cols[h]], mk_ref.at[n][_head_rows(h, n_tok), :].astype(BF16))
              for n, h in pairs]
    probs, invs = [], []
    for s in scores:
        s = s * MEM_SCALE
        p = jnp.exp(s - jnp.max(s, axis=1, keepdims=True))
        invs.append(1.0 / jnp.sum(p, axis=1, keepdims=True))
        probs.append(p.astype(BF16))
    for (n, h), p, inv in zip(pairs, probs, invs):
        mv = mv_ref.at[n][_head_rows(h, n_tok), :].astype(BF16)
        o_ref[rows(n), cols[h]] = (_dot(p, mv) * inv).astype(BF16)


def _attn_prompt_kernel(tq, q_ref, mq_ref, k_ref, vt_ref, mk_ref, mv_ref, omla_ref, omem_ref,
                        m_scr, l_scr, acc_scr):
    i = pl.program_id(1)
    m_scr[...] = jnp.full(m_scr.shape, -jnp.inf, F32)
    l_scr[...] = jnp.zeros(l_scr.shape, F32)
    acc_scr[...] = jnp.zeros(acc_scr.shape, F32)

    shift = CHUNK.bit_length() - 1

    def step(j, ntiles, masked):
        k0 = pl.multiple_of(j * tq, tq)
        nk = ntiles * tq
        scores = []
        for h in range(MLA_HEADS):
            cols = slice(h * HEAD_PAD, (h + 1) * HEAD_PAD)
            scores.append(_dot_nt(k_ref[pl.ds(k0, nk), cols], q_ref[:, cols]))
        if masked:
            k_chunk = (lax.broadcasted_iota(I32, (nk, tq), 0) >> shift) - (ntiles - 1) * (tq >> shift)
            mask = k_chunk <= (lax.broadcasted_iota(I32, (nk, tq), 1) >> shift)
        probs, alphas = [], []
        for h in range(MLA_HEADS):
            s = jnp.where(mask, scores[h], -1e30) if masked else scores[h]
            m_prev = m_scr[h]
            m_new = jnp.maximum(m_prev, jnp.max(s, axis=0, keepdims=True))
            alpha = jnp.exp2(m_prev - m_new)
            p = jnp.exp2(s - m_new)
            l_scr[h] = alpha * l_scr[h] + jnp.sum(p, axis=0, keepdims=True)
            m_scr[h] = m_new
            probs.append(p.astype(BF16))
            alphas.append(alpha)
        for h in range(MLA_HEADS):
            rows = slice(h * V_HEAD, (h + 1) * V_HEAD)
            pv = sum(_dot(vt_ref[0, j + t, rows, :], probs[h][t * tq:(t + 1) * tq, :]) for t in range(ntiles))
            acc_scr[rows, :] = alphas[h] * acc_scr[rows, :] + pv

    def body(jj, carry):
        step(2 * jj, 2, False)
        return carry

    lax.fori_loop(0, i // 2, body, 0)
    pl.when(i % 2 == 0)(lambda: step(i, 1, True))
    pl.when(i % 2 == 1)(lambda: step(i - 1, 2, True))
    o_t = jnp.concatenate([acc_scr[h * V_HEAD:(h + 1) * V_HEAD, :] * (1.0 / l_scr[h]) for h in range(MLA_HEADS)],
                          axis=0)
    omla_ref[...] = o_t.T.astype(BF16)
    _mem_attention(mq_ref, mk_ref, mv_ref, omem_ref)


def _attn_prompt(qp, mq, kp, vt, mk, mv, *, nseq, seq):
    m = qp.shape[0]
    tq = ATT_TQ
    nq = seq // tq
    qrow = lambda w: pl.BlockSpec((tq, w), lambda n, i: (n * nq + i, 0))
    mem = pl.BlockSpec((1,) + mk.shape[1:], lambda n, i: (n, 0, 0))
    out = jax.ShapeDtypeStruct((m, V_ALL), BF16)
    return pl.pallas_call(
        functools.partial(_attn_prompt_kernel, tq),
        out_shape=(out, out),
        grid=(nseq, nq),
        in_specs=[qrow(QK_PAD_W), qrow(mq.shape[1]),
                  pl.BlockSpec((seq, QK_PAD_W), lambda n, i: (n, 0)),
                  pl.BlockSpec((1,) + vt.shape[1:], lambda n, i: (n, 0, 0, 0)),
                  mem, mem],
        out_specs=(qrow(V_ALL),) * 2,
        scratch_shapes=[pltpu.VMEM((MLA_HEADS, 1, tq), F32), pltpu.VMEM((MLA_HEADS, 1, tq), F32),
                        pltpu.VMEM((V_ALL, tq), F32)],
        compiler_params=_params(2),
        name="attn_prompt",
    )(qp, mq, kp, vt, mk, mv)


def _attn_sample_kernel(tq, past, qn_ref, qr_ref, mq_ref, pc_ref, prt_ref, kc_ref, kr_ref, mk_ref, mv_ref,
                        wuk_ref, wuv_ref, omla_ref, omem_ref, qlat, qrot):
    nsq = pc_ref.shape[0]
    seqs = range(nsq)
    for n in seqs:
        tok = slice(n * tq, (n + 1) * tq)
        for h in range(MLA_HEADS):
            qlat[n, h * tq:(h + 1) * tq, :] = _dot(qn_ref[tok, h * QK_NOPE:(h + 1) * QK_NOPE], wuk_ref[h]).astype(BF16)
            qrot[n, h * tq:(h + 1) * tq, :] = qr_ref[tok, h * QK_ROPE:(h + 1) * QK_ROPE]
    keys, scores = [], []
    for n in seqs:
        tok = slice(n * tq, (n + 1) * tq)
        kn, sn = [], []
        for c in range(past // SAMPLE_KV):
            ks = slice(c * SAMPLE_KV, (c + 1) * SAMPLE_KV)
            kc = pc_ref[n, ks, :].astype(BF16)
            kn.append(kc)
            sn.append(_dot_nt(qlat[n], kc) + _dot(qrot[n], prt_ref[n, :, ks].astype(BF16)))
        kn.append(kc_ref[tok, :])
        sn.append(_dot_nt(qlat[n], kc_ref[tok, :]) + _dot_nt(qrot[n], kr_ref[tok, :]))
        keys.append(kn)
        scores.append(sn)
    probs, invs = [], []
    for n in seqs:
        m = functools.reduce(jnp.maximum, [jnp.max(s, axis=1, keepdims=True) for s in scores[n]])
        pn = [jnp.exp(s - m) for s in scores[n]]
        invs.append(1.0 / sum(jnp.sum(p, axis=1, keepdims=True) for p in pn))
        probs.append([p.astype(BF16) for p in pn])
    accs = [sum(_dot(p, kc) for p, kc in zip(probs[n], keys[n])) for n in seqs]
    for n in seqs:
        tok = slice(n * tq, (n + 1) * tq)
        for h in range(MLA_HEADS):
            rows = slice(h * tq, (h + 1) * tq)
            o_lat = (accs[n][rows, :] * invs[n][rows, :]).astype(BF16)
            omla_ref[tok, h * V_HEAD:(h + 1) * V_HEAD] = _dot(o_lat, wuv_ref[h]).astype(BF16)
    _mem_attention(mq_ref, mk_ref, mv_ref, omem_ref)


def _attn_sample(qn, qr, mq, past_c, past_r, kcb, krb, mk, mv, wuk, wuv, *, nseq, seq):
    m = qn.shape[0]
    past = past_c.shape[1]
    nsq = SAMPLE_SEQS
    qrow = lambda w: pl.BlockSpec((nsq * seq, w), lambda n: (n, 0))
    per_seq = lambda a: pl.BlockSpec((nsq,) + a.shape[1:], lambda n: (n, 0, 0))
    out = jax.ShapeDtypeStruct((m, MLA_HEADS * V_HEAD), BF16)
    return pl.pallas_call(
        functools.partial(_attn_sample_kernel, seq, past),
        out_shape=(out, out),
        grid=(nseq // nsq,),
        in_specs=[qrow(Q_NOPE_W), qrow(Q_ROPE_W), qrow(mq.shape[1]), per_seq(past_c), per_seq(past_r),
                  qrow(KV_LORA), qrow(QK_ROPE), per_seq(mk), per_seq(mv), _full(wuk), _full(wuv)],
        out_specs=(qrow(MLA_HEADS * V_HEAD),) * 2,
        scratch_shapes=[pltpu.VMEM((nsq, MLA_HEADS * seq, KV_LORA), BF16),
                        pltpu.VMEM((nsq, MLA_HEADS * seq, QK_ROPE), BF16)],
        compiler_params=_params(),
        name="attn_sample",
    )(qn, qr, mq, past_c, past_r, kcb, krb, mk, mv, wuk, wuv)


ROUTE_ROWS = 8
R_E1, R_E2, R_RANK1, R_RANK2, R_CW1, R_CW2 = range(6)
ROUTER_ROWS = 64
R_GROUP0 = N_EXPERTS


def _split_bf16(x):
    hi = x.astype(BF16)
    lo = (x - hi.astype(F32)).astype(BF16)
    return hi, lo


def _merge_kernel(x_ref, cc_ref, om_ref, ox_ref, wg_ref, bg_ref, wco_ref, wmo_ref, wxo_ref, wout_ref,
                  g1_ref, b1_ref, wr_ref, br_ref, x1_ref, x1b_ref, route_ref, cnt_ref):
    d = x_ref.shape[1]
    tm = x_ref.shape[0]
    x = x_ref[...]
    xb = x.astype(BF16)
    merged = None
    for b, (br_ref_, w_ref_) in enumerate(((cc_ref, wco_ref), (om_ref, wmo_ref), (ox_ref, wxo_ref))):
        gate = _sigmoid(_dot(xb, wg_ref[:, b * d:(b + 1) * d]) + bg_ref[:, b * d:(b + 1) * d])
        term = gate * _dot(br_ref_[...], w_ref_[...])
        merged = term if merged is None else merged + term
    z = DEEPNORM_ALPHA * x + _dot(merged.astype(BF16), wout_ref[...])
    x1 = _layer_norm(z, g1_ref[...], b1_ref[...])
    x1_ref[...] = x1
    x1b_ref[...] = x1.astype(BF16)

    xh, xl = _split_bf16(x1)
    wh, wl = _split_bf16(wr_ref[...])
    lt = _dot_nt(wh, xh) + _dot_nt(wh, xl) + _dot_nt(wl, xh) + br_ref[...]

    g = [lt[R_GROUP0 + k:R_GROUP0 + k + 1, :] for k in range(N_GROUPS)]
    gmax = jnp.maximum(jnp.maximum(g[0], g[1]), jnp.maximum(g[2], g[3]))
    gden = sum(jnp.exp(gk - gmax) for gk in g)
    g_val = 1.0 / gden
    g_idx = jnp.where(g[0] == gmax, 0.0, jnp.where(g[1] == gmax, 1.0, jnp.where(g[2] == gmax, 2.0, 3.0)))
    e_sel = jnp.zeros((EXPERTS_PER_GROUP, tm), F32)
    for k in range(N_GROUPS):
        e_sel = jnp.where(g_idx == float(k), lt[k * EXPERTS_PER_GROUP:(k + 1) * EXPERTS_PER_GROUP, :], e_sel)
    sub = lax.broadcasted_iota(I32, (EXPERTS_PER_GROUP, tm), 0).astype(F32)
    big = float(EXPERTS_PER_GROUP)
    m1 = jnp.max(e_sel, axis=0, keepdims=True)
    i1 = jnp.min(jnp.where(e_sel == m1, sub, big), axis=0, keepdims=True)
    rest = jnp.where(sub == i1, -jnp.inf, e_sel)
    m2 = jnp.max(rest, axis=0, keepdims=True)
    i2 = jnp.min(jnp.where(rest == m2, sub, big), axis=0, keepdims=True)
    eden = jnp.sum(jnp.exp(e_sel - m1), axis=0, keepdims=True)
    p1 = 1.0 / eden
    p2 = jnp.exp(m2 - m1) / eden
    psum = p1 + p2
    cw1 = g_val * (p1 / psum)
    cw2 = g_val * (p2 / psum)
    e1 = g_idx * big + i1
    e2 = g_idx * big + i2

    eio = lax.broadcasted_iota(I32, (N_EXPERTS, tm), 0).astype(F32)
    oh1 = eio == e1
    oh2 = eio == e2
    onehot = jnp.where(oh1, 1.0, jnp.where(oh2, 1.0, 0.0))
    upper = jnp.where(lax.broadcasted_iota(I32, (tm, tm), 0) <= lax.broadcasted_iota(I32, (tm, tm), 1), 1.0, 0.0)
    prefix = _dot(onehot.astype(BF16), upper.astype(BF16))
    rank1 = jnp.sum(jnp.where(oh1, prefix, 0.0), axis=0, keepdims=True) - 1.0
    rank2 = jnp.sum(jnp.where(oh2, prefix, 0.0), axis=0, keepdims=True) - 1.0
    route_ref[R_E1:R_E1 + 1, :] = e1
    route_ref[R_E2:R_E2 + 1, :] = e2
    route_ref[R_RANK1:R_RANK1 + 1, :] = rank1
    route_ref[R_RANK2:R_RANK2 + 1, :] = rank2
    route_ref[R_CW1:R_CW1 + 1, :] = cw1
    route_ref[R_CW2:R_CW2 + 1, :] = cw2
    route_ref[R_CW2 + 1:ROUTE_ROWS, :] = jnp.zeros((ROUTE_ROWS - R_CW2 - 1, tm), F32)
    cnt_ref[0] = jnp.broadcast_to(jnp.sum(onehot, axis=1, keepdims=True), (N_EXPERTS, LANES))


def _merge(x2d, cconv, omla, omem, consts):
    m, d = x2d.shape
    tm = TOKEN_TILE
    nt = m // tm
    row = lambda w: pl.BlockSpec((tm, w), lambda i: (i, 0))
    sds = jax.ShapeDtypeStruct
    return pl.pallas_call(
        _merge_kernel,
        out_shape=(sds((m, d), F32), sds((m, d), BF16), sds((ROUTE_ROWS, m), F32),
                   sds((nt, N_EXPERTS, LANES), F32)),
        grid=(nt,),
        in_specs=[row(d), row(cconv.shape[1]), row(omla.shape[1]), row(omem.shape[1])] + [_full(c) for c in consts],
        out_specs=(row(d), row(d), pl.BlockSpec((ROUTE_ROWS, tm), lambda i: (0, i)),
                   pl.BlockSpec((1, N_EXPERTS, LANES), lambda i: (i, 0, 0))),
        compiler_params=_params(),
        name="merge",
    )(x2d, cconv, omla, omem, *consts)


def _local_positions(route_ref, lsb_ref):
    tm = route_ref.shape[1]
    lsb = lsb_ref[0]
    eio = lax.broadcasted_iota(I32, (N_EXPERTS, tm), 0).astype(F32)
    pos = []
    for r_e, r_rank in ((R_E1, R_RANK1), (R_E2, R_RANK2)):
        e = route_ref[r_e:r_e + 1, :]
        start = jnp.sum(jnp.where(eio == e, lsb, 0.0), axis=0, keepdims=True)
        pos.append((start + route_ref[r_rank:r_rank + 1, :]).astype(I32))
    return pos


def _segment_copies(i, ls_ref, nch_ref, gd_ref, make_copy):
    def seg(e, carry):
        k = i * N_EXPERTS + e
        ls = ls_ref[k]
        gd = gd_ref[k]

        def chunk(c, carry2):
            make_copy(pl.multiple_of(ls + c * ROW_CHUNK, ROW_CHUNK),
                      pl.multiple_of(gd + c * ROW_CHUNK, ROW_CHUNK)).start()
            return carry2

        return lax.fori_loop(0, nch_ref[k], chunk, carry)

    lax.fori_loop(0, N_EXPERTS, seg, 0)


def _zero_fill_copies(ztab_ref, zero_scr, buf_ref, zsem, wait):
    def run(rows, start_of, count_of, n_groups):
        def group(g, carry):
            start = start_of(g)

            def one(c, carry2):
                cp = pltpu.make_async_copy(zero_scr.at[pl.ds(0, rows)],
                                           buf_ref.at[pl.ds(pl.multiple_of(start + c * rows, ROW_CHUNK), rows)], zsem)
                cp.wait() if wait else cp.start()
                return carry2

            return lax.fori_loop(0, count_of(g), one, carry)

        lax.fori_loop(0, n_groups, group, 0)

    n = N_EXPERTS
    run(ZERO_MID, lambda e: ztab_ref[e], lambda e: ztab_ref[n + e], n)
    run(ROW_CHUNK, lambda e: ztab_ref[2 * n + e], lambda e: ztab_ref[3 * n + e], n)
    run(EXPERT_TILE, lambda g: ztab_ref[4 * n], lambda g: ztab_ref[4 * n + 1], 1)


def _dispatch_kernel(nt_p, ls_ref, nch_ref, gd_ref, ntot_ref, ztab_ref, xp_ref, xs_ref, route_ref, lsb_ref,
                     buf_ref, xs_scr, zero_scr, sem, zsem):
    i = pl.program_id(0)
    last = pl.num_programs(0) - 1
    slot = i % 2
    tm = xp_ref.shape[0]

    @pl.when(i == 0)
    def _():
        zero_scr[...] = jnp.zeros(zero_scr.shape, BF16)
        _zero_fill_copies(ztab_ref, zero_scr, buf_ref, zsem, wait=False)

    lp1, lp2 = _local_positions(route_ref, lsb_ref)
    rowi = lax.broadcasted_iota(I32, (SORT_ROWS, tm), 0)
    sel1 = rowi == lp1
    sel2 = rowi == lp2
    perm = jnp.where(sel1, 1.0, jnp.where(sel2, 1.0, 0.0)).astype(BF16)

    d = xp_ref.shape[1]

    def sort_rows(x_ref):
        xs_scr[slot, :, 0:d] = _dot(perm, x_ref[...]).astype(BF16)

    pl.when(i < nt_p)(lambda: sort_rows(xp_ref))
    pl.when(i >= nt_p)(lambda: sort_rows(xs_ref))
    cw = jnp.sum(jnp.where(sel1, route_ref[R_CW1:R_CW1 + 1, :],
                           jnp.where(sel2, route_ref[R_CW2:R_CW2 + 1, :], 0.0)), axis=1, keepdims=True)
    lane = lax.broadcasted_iota(I32, (SORT_ROWS, CW_LANES), 1)
    cw_block = jnp.zeros((SORT_ROWS, CW_LANES), F32)
    rem = cw
    for part in range(CW_PARTS):
        piece = rem.astype(BF16).astype(F32)
        cw_block = jnp.where(lane == part, piece, cw_block)
        rem = rem - piece
    xs_scr[slot, :, d:d + CW_LANES] = cw_block.astype(BF16)

    def make_copy(s):
        def f(src_row, dst_row):
            return pltpu.make_async_copy(xs_scr.at[s, pl.ds(src_row, ROW_CHUNK)],
                                         buf_ref.at[pl.ds(dst_row, ROW_CHUNK)], sem.at[s])
        return f

    def drain(step, s):
        def body(c, carry):
            make_copy(s)(0, 0).wait()
            return carry
        lax.fori_loop(0, ntot_ref[step], body, 0)

    _segment_copies(i, ls_ref, nch_ref, gd_ref, make_copy(slot))
    pl.when(i > 0)(lambda: drain(i - 1, 1 - slot))

    @pl.when(i == last)
    def _():
        drain(i, slot)
        _zero_fill_copies(ztab_ref, zero_scr, buf_ref, zsem, wait=True)


def _dispatch(seg, ztab, x1b_p, x1b_s, route, lsb, buf_rows):
    d = x1b_p.shape[1]
    tm = TOKEN_TILE
    nt_p, nt_s = x1b_p.shape[0] // tm, x1b_s.shape[0] // tm
    width = d + CW_LANES
    grid_spec = pltpu.PrefetchScalarGridSpec(
        num_scalar_prefetch=5,
        grid=(nt_p + nt_s,),
        in_specs=[pl.BlockSpec((tm, d), lambda i, *_: (jnp.minimum(i, nt_p - 1), 0)),
                  pl.BlockSpec((tm, d), lambda i, *_: (jnp.maximum(i - nt_p, 0), 0)),
                  pl.BlockSpec((ROUTE_ROWS, tm), lambda i, *_: (0, i)),
                  pl.BlockSpec((1, N_EXPERTS, tm), lambda i, *_: (i, 0, 0))],
        out_specs=pl.BlockSpec(memory_space=pl.ANY),
        scratch_shapes=[pltpu.VMEM((2, SORT_ROWS, width), BF16), pltpu.VMEM((EXPERT_TILE, width), BF16),
                        pltpu.SemaphoreType.DMA((2,)), pltpu.SemaphoreType.DMA(())],
    )
    return pl.pallas_call(
        functools.partial(_dispatch_kernel, nt_p),
        out_shape=jax.ShapeDtypeStruct((buf_rows, width), BF16),
        grid_spec=grid_spec,
        compiler_params=_params(),
        name="dispatch",
    )(*seg, ztab, x1b_p, x1b_s, route, lsb)


def _experts_kernel(te_ref, nu_ref, xs_ref, w1_ref, w3_ref, w2_ref, y_ref):
    @pl.when(pl.program_id(0) >= nu_ref[0])
    def _():
        y_ref[...] = jnp.zeros(y_ref.shape, BF16)

    @pl.when(pl.program_id(0) < nu_ref[0])
    def _():
        d = y_ref.shape[1]
        x = xs_ref[:, 0:d]
        cw = jnp.sum(xs_ref[:, d:d + CW_LANES].astype(F32), axis=1, keepdims=True)
        h1 = _dot(x, w1_ref[0].astype(BF16))
        h3 = _dot(x, w3_ref[0].astype(BF16))
        h = (h1 * _sigmoid(h1) * h3 * cw).astype(BF16)
        y_ref[...] = _dot(h, w2_ref[0].astype(BF16)).astype(BF16)


def _experts(te, nu, buf, w1, w3, w2):
    rows, width = buf.shape
    tr = EXPERT_TILE
    clamp = lambda i, te_ref, nu_ref: (jnp.minimum(i, nu_ref[0] - 1), 0)
    wspec = lambda w: pl.BlockSpec((1,) + w.shape[1:], lambda i, te_ref, nu_ref: (te_ref[i], 0, 0))
    grid_spec = pltpu.PrefetchScalarGridSpec(
        num_scalar_prefetch=2,
        grid=(rows // tr,),
        in_specs=[pl.BlockSpec((tr, width), clamp), wspec(w1), wspec(w3), wspec(w2)],
        out_specs=pl.BlockSpec((tr, width - CW_LANES), lambda i, te_ref, nu_ref: (i, 0)),
    )
    return pl.pallas_call(
        _experts_kernel,
        out_shape=jax.ShapeDtypeStruct((rows, width - CW_LANES), BF16),
        grid_spec=grid_spec,
        compiler_params=_params(),
        name="experts",
    )(te, nu, buf, w1, w3, w2)


def _combine_kernel(ls_ref, nch_ref, gd_ref, ntot_ref, y_hbm, route_ref, lsb_ref, x1_ref, g2_ref, b2_ref,
                    out_ref, ys_scr, sem):
    i = pl.program_id(0)
    last = pl.num_programs(0) - 1
    slot = i % 2
    tm = x1_ref.shape[0]

    def make_copy(s):
        def f(src_row, dst_row):
            return pltpu.make_async_copy(y_hbm.at[pl.ds(dst_row, ROW_CHUNK)],
                                         ys_scr.at[s, pl.ds(src_row, ROW_CHUNK)], sem.at[s])
        return f

    def fetch(step, s):
        ys_scr[s] = jnp.zeros(ys_scr.shape[1:], BF16)
        _segment_copies(step, ls_ref, nch_ref, gd_ref, make_copy(s))

    pl.when(i == 0)(lambda: fetch(i, slot))
    pl.when(i < last)(lambda: fetch(i + 1, 1 - slot))
    lp1, lp2 = _local_positions(route_ref, lsb_ref)
    rowi = lax.broadcasted_iota(I32, (SORT_ROWS, tm), 0)
    perm = jnp.where(rowi == lp1, 1.0, jnp.where(rowi == lp2, 1.0, 0.0)).astype(BF16)

    def drain(c, carry):
        make_copy(slot)(0, 0).wait()
        return carry

    lax.fori_loop(0, ntot_ref[i], drain, 0)
    y = _dot_tn(perm, ys_scr[slot])
    out_ref[...] = _layer_norm(DEEPNORM_ALPHA * x1_ref[...] + y, g2_ref[...], b2_ref[...])


def _combine(seg, ybuf, route, lsb, x1, g2, b2):
    m, d = x1.shape
    tm = TOKEN_TILE
    nt = m // tm
    grid_spec = pltpu.PrefetchScalarGridSpec(
        num_scalar_prefetch=4,
        grid=(nt,),
        in_specs=[pl.BlockSpec(memory_space=pl.ANY),
                  pl.BlockSpec((ROUTE_ROWS, tm), lambda i, *_: (0, i)),
                  pl.BlockSpec((1, N_EXPERTS, tm), lambda i, *_: (i, 0, 0)),
                  pl.BlockSpec((tm, d), lambda i, *_: (i, 0)),
                  pl.BlockSpec(g2.shape, lambda i, *_: (0, 0)),
                  pl.BlockSpec(b2.shape, lambda i, *_: (0, 0))],
        out_specs=pl.BlockSpec((tm, d), lambda i, *_: (i, 0)),
        scratch_shapes=[pltpu.VMEM((2, SORT_ROWS, d), BF16), pltpu.SemaphoreType.DMA((2,))],
    )
    return pl.pallas_call(
        _combine_kernel,
        out_shape=jax.ShapeDtypeStruct((m, d), F32),
        grid_spec=grid_spec,
        compiler_params=_params(),
        name="combine",
    )(*seg, ybuf, route, lsb, x1, g2, b2)


def _rope_tables(pos):
    half = QK_ROPE // 2
    inv = jnp.power(ROPE_BASE, -jnp.arange(half, dtype=F32) / half)
    ang = pos.astype(F32)[:, None] * inv[None, :]
    cos, sin = jnp.cos(ang), jnp.sin(ang)
    ck = jnp.concatenate([cos, cos], axis=1)
    sk = jnp.concatenate([-sin, sin], axis=1)
    cq = jnp.tile(ck, (1, MLA_HEADS)) * MLA_SCALE
    sq = jnp.tile(sk, (1, MLA_HEADS)) * MLA_SCALE
    return ck, sk, cq, sq


def _rope_tables_padded(pos):
    ck, sk, _, _ = _rope_tables(pos)
    n = pos.shape[0]
    s2 = MLA_SCALE * 1.4426950408889634
    z = lambda w: jnp.zeros((n, w), F32)
    tail = HEAD_PAD - QK_NOPE - QK_ROPE
    ca = jnp.concatenate([jnp.full((n, QK_NOPE), s2, F32), ck * s2, z(tail)], axis=1)
    sb = jnp.concatenate([z(QK_NOPE), sk * s2, z(tail)], axis=1)
    ckp = jnp.concatenate([z(QK_NOPE), ck, z(tail)], axis=1)
    skp = jnp.concatenate([z(QK_NOPE), sk, z(tail)], axis=1)
    return ca, sb, ckp, skp


def _ceil_to(x, k):
    return (x + k - 1) // k * k


def _segment_tables(cnt, n_row_tiles):
    cnt8 = _ceil_to(cnt, ROW_CHUNK)
    lstart = jnp.cumsum(cnt8, axis=1) - cnt8
    filled = jnp.sum(cnt8, axis=0)
    tot = _ceil_to(filled, EXPERT_TILE)
    ends = jnp.cumsum(tot)
    gdst = (ends - tot)[None, :] + jnp.cumsum(cnt8, axis=0) - cnt8
    nch = cnt8 // ROW_CHUNK
    n_used = ends[-1] // EXPERT_TILE
    tile_row = jnp.minimum(jnp.arange(n_row_tiles, dtype=I32), n_used - 1) * EXPERT_TILE
    tile_expert = jnp.minimum(jnp.sum((ends[None, :] <= tile_row[:, None]).astype(I32), axis=1), N_EXPERTS - 1)
    tail0 = ends - tot + filled
    tail = tot - filled
    n_mid = tail // ZERO_MID
    ztab = jnp.concatenate([tail0, n_mid, tail0 + n_mid * ZERO_MID, (tail % ZERO_MID) // ROW_CHUNK,
                            ends[-1:], (n_row_tiles - n_used).reshape(1)]).astype(I32)
    return lstart, gdst, nch, jnp.sum(nch, axis=1), tile_expert, n_used.reshape(1).astype(I32), ztab


def kernel(x_prompt, x_sample, cache_conv, cache_mla_latent, cache_mla_krope, cache_mem_k, cache_mem_v,
           mem_prompt, w_in, b_gate, w_dw, b_dw, g_cn, b_cn, w_co, g_qa, w_uq, g_kva, w_uk, w_uv, w_mo,
           w_mem_k, w_mem_v, w_mem_o, w_out, g_ln1, b_ln1, w_group, b_group, w_router, b_router,
           w_e1, w_e3, w_e2, g_ln2, b_ln2):
    n_p, s_p, d = x_prompt.shape
    n_s, s_s, _ = x_sample.shape
    past = cache_mla_latent.shape[2]
    n_mem = mem_prompt.shape[1]
    assert w_in.shape[0] == DEPTH == 1
    assert s_p % TOKEN_TILE == 0 and s_p % ATT_TQ == 0 and TOKEN_TILE % s_s == 0 and (n_s * s_s) % TOKEN_TILE == 0
    assert s_s <= CHUNK and past % CHUNK == 0 and past % SAMPLE_KV == 0
    assert n_s % SAMPLE_SEQS == 0
    mem_w = MEM_HEADS * MEM_HEAD_DIM
    row2 = lambda v: v.reshape(1, -1)

    wl = w_in[0]
    sp = (C_GLU, C_GLU + Q_LORA, C_GLU + Q_LORA + KV_LORA + QK_ROPE, C_GLU + Q_LORA + KV_LORA + QK_ROPE + mem_w)
    w_glu, w_qd, w_kvd, w_mq, w_gate = wl[:, :sp[0]], wl[:, sp[0]:sp[1]], wl[:, sp[1]:sp[2]], wl[:, sp[2]:sp[3]], wl[:, sp[3]:]
    half = QK_ROPE // 2
    w_kr = w_kvd[:, KV_LORA:]
    w_krs = jnp.concatenate([w_kr[:, half:], w_kr[:, :half]], axis=1)
    pad = jnp.zeros((d, C_MQ - (C_KRS + QK_ROPE)), F32)
    w_a = jnp.concatenate([w_glu, w_qd, w_kvd, w_krs, pad, w_mq], axis=1).astype(BF16)
    wq = w_uq[0].reshape(Q_LORA, MLA_HEADS, QK_NOPE + QK_ROPE)
    wq_r = wq[:, :, QK_NOPE:]
    wq_rs = jnp.concatenate([wq_r[:, :, half:], wq_r[:, :, :half]], axis=2)
    w_uq2 = jnp.concatenate([wq[:, :, :QK_NOPE].reshape(Q_LORA, -1), wq_r.reshape(Q_LORA, -1),
                             wq_rs.reshape(Q_LORA, -1)], axis=1).astype(BF16)
    wdw8 = jnp.broadcast_to(w_dw[0][:, None, :], (CONV_WIDTH, SUBLANES, CONV_CH))
    in_consts = (w_a, w_uq2, wdw8, row2(b_dw[0]), row2(g_cn[0]), row2(b_cn[0]), row2(g_qa[0]), row2(g_kva[0]))
    wuk_t = jnp.transpose(w_uk[0], (1, 2, 0)).astype(BF16)
    wuv_h = jnp.transpose(w_uv[0], (1, 0, 2)).astype(BF16)
    w_r = jnp.zeros((ROUTER_ROWS, d), F32).at[:N_EXPERTS].set(w_router[0].T).at[R_GROUP0:R_GROUP0 + N_GROUPS].set(w_group[0].T)
    b_r = jnp.zeros((ROUTER_ROWS, 1), F32).at[:N_EXPERTS, 0].set(b_router[0]).at[R_GROUP0:R_GROUP0 + N_GROUPS, 0].set(b_group[0])
    merge_consts = (w_gate.astype(BF16), row2(b_gate[0]), w_co[0].astype(BF16), w_mo[0].astype(BF16),
                    w_mem_o[0].astype(BF16), w_out[0].astype(BF16), row2(g_ln1[0]), row2(b_ln1[0]), w_r, b_r)

    zc = lambda n: jnp.zeros((d, n), F32)
    lane_pad = HEAD_PAD - QK_NOPE - QK_ROPE
    w_a_p = jnp.concatenate([w_glu, w_qd, w_kvd[:, :KV_LORA], zc(QK_NOPE), w_kr, zc(lane_pad),
                             zc(QK_NOPE), w_krs, zc(lane_pad), w_mq], axis=1).astype(BF16)
    zq = lambda n: jnp.zeros((Q_LORA, MLA_HEADS, n), F32)
    w_qa = jnp.concatenate([wq, zq(lane_pad)], axis=2).reshape(Q_LORA, QK_PAD_W).astype(BF16)
    w_qb = jnp.concatenate([zq(QK_NOPE), wq_rs, zq(lane_pad)], axis=2).reshape(Q_LORA, QK_PAD_W).astype(BF16)
    w_ukp = jnp.concatenate([w_uk[0], jnp.zeros((KV_LORA, MLA_HEADS, HEAD_PAD - QK_NOPE), F32)],
                            axis=2).reshape(KV_LORA, QK_PAD_W).astype(BF16)
    w_uvt = w_uv[0].reshape(KV_LORA, V_ALL).T.astype(BF16)
    in_consts_p = (w_a_p, w_qa, w_qb, w_ukp, w_uvt) + in_consts[2:]

    xp = x_prompt.reshape(n_p * s_p, d)
    mk_p, mv_p = _mem_proj(mem_prompt, w_mem_k[0].astype(BF16), w_mem_v[0].astype(BF16))
    (cc_p, nconv_p, ckv_p, krot_p, qp_p, kp_p, vt_p, mq_p) = _inproj(
        xp, None, in_consts_p, _rope_tables_padded(jnp.arange(s_p, dtype=I32)),
        nseg=1, seg_len=TOKEN_TILE, tiles_per_seq=s_p // TOKEN_TILE, nseq=n_p)
    omla_p, omem_p = _attn_prompt(qp_p, mq_p, kp_p, vt_p, mk_p, mv_p, nseq=n_p, seq=s_p)
    x1_p, x1b_p, route_p, cnt_p = _merge(xp, cc_p, omla_p, omem_p, merge_consts)

    xs = x_sample.reshape(n_s * s_s, d)
    seqs_per_tile = TOKEN_TILE // s_s
    tabs_s = tuple(jnp.tile(t, (seqs_per_tile, 1)) for t in _rope_tables(past + jnp.arange(s_s, dtype=I32)))
    (cc_s, nconv_s, ckv_s, kcb_s, krot_s, krb_s, qn_s, qr_s, mq_s) = _inproj(
        xs, cache_conv[0], in_consts, tabs_s, nseg=seqs_per_tile, seg_len=s_s, tiles_per_seq=1, nseq=n_s)
    omla_s, omem_s = _attn_sample(qn_s, qr_s, mq_s, cache_mla_latent[0], jnp.swapaxes(cache_mla_krope[0], 1, 2), kcb_s, krb_s,
                                  cache_mem_k[0].reshape(n_s, n_mem * MEM_HEADS, MEM_HEAD_DIM),
                                  cache_mem_v[0].reshape(n_s, n_mem * MEM_HEADS, MEM_HEAD_DIM),
                                  wuk_t, wuv_h, nseq=n_s, seq=s_s)
    x1_s, x1b_s, route_s, cnt_s = _merge(xs, cc_s, omla_s, omem_s, merge_consts)

    nt_p, nt_s = cnt_p.shape[0], cnt_s.shape[0]
    cnt = jnp.concatenate([cnt_p[:, :, 0], cnt_s[:, :, 0]], axis=0).astype(I32)
    m_tot = (nt_p + nt_s) * TOKEN_TILE
    buf_rows = _ceil_to(2 * m_tot + (nt_p + nt_s) * N_EXPERTS * (ROW_CHUNK - 1) + N_EXPERTS * (EXPERT_TILE - 1), EXPERT_TILE)
    lstart, gdst, nch, ntot, tile_expert, n_used, ztab = _segment_tables(cnt, buf_rows // EXPERT_TILE)
    lsb = jnp.broadcast_to(lstart.astype(F32)[:, :, None], lstart.shape + (TOKEN_TILE,))
    seg = lambda a, b: (lstart[a:b].reshape(-1), nch[a:b].reshape(-1), gdst[a:b].reshape(-1), ntot[a:b])
    seg_p, seg_s = seg(0, nt_p), seg(nt_p, nt_p + nt_s)
    buf = _dispatch(seg(0, nt_p + nt_s), ztab, x1b_p, x1b_s, jnp.concatenate([route_p, route_s], axis=1), lsb, buf_rows)
    n_e = N_EXPERTS
    ybuf = _experts(tile_expert, n_used, buf, w_e1[0].reshape((n_e,) + w_e1.shape[3:]),
                    w_e3[0].reshape((n_e,) + w_e3.shape[3:]), w_e2[0].reshape((n_e,) + w_e2.shape[3:]))
    g2, b2 = row2(g_ln2[0]), row2(b_ln2[0])
    y_p = _combine(seg_p, ybuf, route_p, lsb[:nt_p], x1_p, g2, b2)
    y_s = _combine(seg_s, ybuf, route_s, lsb[nt_p:], x1_s, g2, b2)

    lead = lambda a, n, t: a.reshape((1, n, t) + a.shape[1:])
    return (y_p.reshape(n_p, s_p, d), y_s.reshape(n_s, s_s, d),
            nconv_p[None], lead(ckv_p, n_p, s_p), lead(krot_p, n_p, s_p),
            mk_p.reshape(1, n_p, n_mem, MEM_HEADS, MEM_HEAD_DIM), mv_p.reshape(1, n_p, n_mem, MEM_HEADS, MEM_HEAD_DIM),
            nconv_s[None], lead(ckv_s, n_s, s_s), lead(krot_s, n_s, s_s))
```
